```python
import math
import jax
import jax.numpy as jnp
from jax import lax
import numpy as np

D_MODEL = 1024
BATCH = 4
SEQ = 4096
DEPTH = 4
DEC_BATCH = 16
DEC_SEQ = 32
PAST_LEN = 2048

CHUNK = 64
Q_BLOCK = 128
N_MIXERS = 3
N_A_LAYERS = (DEPTH + 2) // 3
N_B_LAYERS = (DEPTH + 1) // 3
N_C_LAYERS = DEPTH // 3
ROPE_THETA = 500000.0
EPS = 1e-6
D_SUB = 64
H_A = D_MODEL // (2 * D_SUB)
ROT_A = D_SUB // 4
H_B = 4
DQK_B = D_MODEL // (2 * H_B)
DV_B = D_MODEL // H_B
FORGET_BIAS = 3.0
H_C = 8
NOPE_C = 128
ROPE_C = 64
V_C = 128
Q_LORA = 384
KV_LORA = 256
D_FF = -(-8 * D_MODEL // (3 * 256)) * 256

kernel_name = 'hybrid_streaming_encoder_step'


def rmsnorm(x, g):
    xf = x.astype(jnp.float32)
    y = xf * lax.rsqrt(jnp.mean(xf * xf, axis=-1, keepdims=True) + EPS)
    return (y * g.astype(jnp.float32)).astype(x.dtype)


def rope(x, pos, rot):
    half = rot // 2
    inv = jnp.power(jnp.float32(ROPE_THETA), -jnp.arange(half, dtype=jnp.float32) * (2.0 / rot))
    ang = pos.astype(jnp.float32)[:, None] * inv[None, :]
    cos = jnp.cos(ang)[:, None, :]
    sin = jnp.sin(ang)[:, None, :]
    xr = x[..., :rot].astype(jnp.float32)
    x1, x2 = xr[..., :half], xr[..., half:]
    out = jnp.concatenate([x1 * cos - x2 * sin, x2 * cos + x1 * sin], axis=-1).astype(x.dtype)
    return jnp.concatenate([out, x[..., rot:]], axis=-1)


def chunk_mask(pos_q, pos_k):
    return (pos_k[None, :] // CHUNK) <= (pos_q[:, None] // CHUNK)


def sweep_queries(fn, qs, pos_q):
    T = pos_q.shape[0]
    if T <= Q_BLOCK:
        return fn(qs, pos_q)
    nb = T // Q_BLOCK
    qs_b = tuple(jnp.moveaxis(q.reshape(q.shape[0], nb, Q_BLOCK, *q.shape[2:]), 1, 0) for q in qs)
    out = lax.map(lambda a: fn(a[0], a[1]), (qs_b, pos_q.reshape(nb, Q_BLOCK)))
    out = jnp.moveaxis(out, 0, 1)
    return out.reshape(out.shape[0], T, *out.shape[3:])


def diff_attention(h, past_k, past_v, w_qkv, lam_vecs, g_sub, w_o, lambda_init):
    B, T, _ = h.shape
    P = 0 if past_k is None else past_k.shape[1]
    pos_q = P + jnp.arange(T, dtype=jnp.int32)
    pos_k = jnp.arange(P + T, dtype=jnp.int32)
    q, k, v = jnp.split(h @ w_qkv, 3, axis=-1)
    q = rope(q.reshape(B, T, 2 * H_A, D_SUB), pos_q, ROT_A).reshape(B, T, H_A, 2, D_SUB)
    k_row = rope(k.reshape(B, T, 2 * H_A, D_SUB), pos_q, ROT_A).reshape(B, T, H_A, 2 * D_SUB)
    v_row = v.reshape(B, T, H_A, 2 * D_SUB)
    k_all = k_row if past_k is None else jnp.concatenate([past_k.astype(k_row.dtype), k_row], axis=1)
    v_all = v_row if past_v is None else jnp.concatenate([past_v.astype(v_row.dtype), v_row], axis=1)
    k1, k2 = k_all[..., :D_SUB], k_all[..., D_SUB:]
    lv = lam_vecs.astype(jnp.float32)
    lam = jnp.exp(jnp.sum(lv[0] * lv[1])) - jnp.exp(jnp.sum(lv[2] * lv[3])) + lambda_init
    scale = D_SUB ** -0.5

    def block(qs, pq):
        q1, q2 = qs
        mask = chunk_mask(pq, pos_k)

        def probs(qq, kk):
            s = jnp.einsum('bqhd,bkhd->bhqk', qq, kk, preferred_element_type=jnp.float32) * scale
            return jax.nn.softmax(jnp.where(mask, s, -jnp.inf), axis=-1)

        p = probs(q1, k1) - lam * probs(q2, k2)
        return jnp.einsum('bhqk,bkhd->bqhd', p.astype(v_all.dtype), v_all)

    o = sweep_queries(block, (q[:, :, :, 0], q[:, :, :, 1]), pos_q)
    o = rmsnorm(o, g_sub) * (1.0 - lambda_init)
    return o.reshape(B, T, D_MODEL) @ w_o, k_row, v_row


def to_blocks(a, nb, L):
    B, T, H = a.shape[:3]
    a = a.reshape(B, nb, L, H, *a.shape[3:])
    return jnp.moveaxis(a, (1, 3), (0, 2))


def mlstm(h, c0, n0, m0, w_in, b_gates, g_out, w_out):
    B, T, _ = h.shape
    nqk, nv = H_B * DQK_B, H_B * DV_B
    f32 = jnp.float32
    proj = h @ w_in
    q = proj[..., :nqk].reshape(B, T, H_B, DQK_B).astype(f32)
    k = proj[..., nqk:2 * nqk].reshape(B, T, H_B, DQK_B).astype(f32) * (DQK_B ** -0.5)
    v = proj[..., 2 * nqk:2 * nqk + nv].reshape(B, T, H_B, DV_B).astype(f32)
    o_pre = proj[..., 2 * nqk + nv:2 * nqk + 2 * nv]
    gates = proj[..., 2 * nqk + 2 * nv:].astype(f32) + b_gates.astype(f32)
    ig = gates[..., :H_B]
    lf = jax.nn.log_sigmoid(gates[..., H_B:])
    L = CHUNK if T % CHUNK == 0 else T
    nb = T // L
    tril = jnp.tril(jnp.ones((L, L), dtype=bool))

    def step(carry, blk):
        C, n, m = carry
        qb, kb, vb, ib, fb = blk
        b = jnp.cumsum(fb, axis=-1)
        g = b[..., -1]
        dm = jnp.where(tril, b[..., :, None] - b[..., None, :] + ib[..., None, :], -jnp.inf)
        inter = b + m[..., None]
        m_t = jnp.maximum(inter, jnp.max(dm, axis=-1))
        w_ts = jnp.exp(dm - m_t[..., None])
        a = jnp.exp(inter - m_t)
        qk = jnp.einsum('bhtd,bhsd->bhts', qb, kb) * w_ts
        num = a[..., None] * jnp.einsum('bhtd,bhdv->bhtv', qb, C) + jnp.einsum('bhts,bhsv->bhtv', qk, vb)
        den = a * jnp.einsum('bhtd,bhd->bht', qb, n) + jnp.sum(qk, axis=-1)
        h_out = num / jnp.maximum(jnp.abs(den), jnp.exp(-m_t))[..., None]
        r = g[..., None] - b + ib
        m_new = jnp.maximum(g + m, jnp.max(r, axis=-1))
        wr = jnp.exp(r - m_new[..., None])
        decay = jnp.exp(g + m - m_new)
        kw = kb * wr[..., None]
        C_new = decay[..., None, None] * C + jnp.einsum('bhsd,bhsv->bhdv', kw, vb)
        n_new = decay[..., None] * n + jnp.sum(kw, axis=2)
        return (C_new, n_new, m_new), h_out

    carry0 = (c0.astype(f32), n0.astype(f32), m0.astype(f32))
    blocks = tuple(to_blocks(t, nb, L) for t in (q, k, v, ig, lf))
    (cT, nT, mT), hs = lax.scan(step, carry0, blocks)
    hs = jnp.moveaxis(hs, (0, 2), (1, 3)).reshape(B, T, H_B, DV_B)
    hn = rmsnorm(hs, g_out.reshape(H_B, DV_B)).astype(h.dtype).reshape(B, T, nv)
    out = (jax.nn.sigmoid(o_pre) * hn) @ w_out
    dt = h.dtype
    return out, cT.astype(dt), nT.astype(dt), mT.astype(dt)


def mla(h, past_kv, past_kr, w_dq, g_q, w_uq, w_dkv, g_kv, w_ukv, w_o):
    B, T, _ = h.shape
    P = 0 if past_kv is None else past_kv.shape[1]
    pos_q = P + jnp.arange(T, dtype=jnp.int32)
    pos_k = jnp.arange(P + T, dtype=jnp.int32)
    q = (rmsnorm(h @ w_dq, g_q) @ w_uq).reshape(B, T, H_C, NOPE_C + ROPE_C)
    q_nope = q[..., :NOPE_C]
    q_rope = rope(q[..., NOPE_C:], pos_q, ROPE_C)
    dkv = h @ w_dkv
    kv_row = rmsnorm(dkv[..., :KV_LORA], g_kv)
    kr_row = rope(dkv[..., KV_LORA:][:, :, None, :], pos_q, ROPE_C)[:, :, 0, :]
    kv_all = kv_row if past_kv is None else jnp.concatenate([past_kv.astype(kv_row.dtype), kv_row], axis=1)
    kr_all = kr_row if past_kr is None else jnp.concatenate([past_kr.astype(kr_row.dtype), kr_row], axis=1)
    kv = (kv_all @ w_ukv).reshape(B, P + T, H_C, NOPE_C + V_C)
    k_nope, v = kv[..., :NOPE_C], kv[..., NOPE_C:]
    scale = (NOPE_C + ROPE_C) ** -0.5

    def block(qs, pq):
        qn, qr = qs
        mask = chunk_mask(pq, pos_k)
        s = (jnp.einsum('bqhd,bkhd->bhqk', qn, k_nope, preferred_element_type=jnp.float32)
             + jnp.einsum('bqhr,bkr->bhqk', qr, kr_all, preferred_element_type=jnp.float32)) * scale
        p = jax.nn.softmax(jnp.where(mask, s, -jnp.inf), axis=-1)
        return jnp.einsum('bhqk,bkhd->bqhd', p.astype(v.dtype), v)

    o = sweep_queries(block, (q_nope, q_rope), pos_q)
    return o.reshape(B, T, H_C * V_C) @ w_o, kv_row, kr_row


def swiglu(h, w_in, w_out):
    a, b = jnp.split(h @ w_in, 2, axis=-1)
    return (jax.nn.silu(a) * b) @ w_out


def trunk(x, c, past, p):
    B = x.shape[0]
    a_k, a_v, b_c, b_n, b_m, c_kv, c_kr = [], [], [], [], [], [], []
    for i in range(DEPTH):
        kind, j = i % N_MIXERS, i // N_MIXERS
        mod = (jax.nn.silu(c) @ p['w_ada'][i] + p['b_ada'][i])[:, None, :]
        sh1, sc1, gt1, sh2, sc2, gt2 = jnp.split(mod, 6, axis=-1)
        h = rmsnorm(x, p['g_norm1'][i]) * (1.0 + sc1) + sh1
        if kind == 0:
            pk, pv = (None, None) if past is None else (past[0][j], past[1][j])
            out, kr, vr = diff_attention(h, pk, pv, p['w_a_qkv'][j], p['a_lambda'][j], p['g_a_sub'][j],
                                         p['w_a_o'][j], 0.8 - 0.6 * math.exp(-0.3 * i))
            a_k.append(kr)
            a_v.append(vr)
        elif kind == 1:
            if past is None:
                c0 = jnp.zeros((B, H_B, DQK_B, DV_B), jnp.float32)
                n0 = jnp.zeros((B, H_B, DQK_B), jnp.float32)
                m0 = jnp.zeros((B, H_B), jnp.float32)
            else:
                c0, n0, m0 = past[2][j], past[3][j], past[4][j]
            out, cT, nT, mT = mlstm(h, c0, n0, m0, p['w_b_in'][j], p['b_b_gates'][j], p['g_b_out'][j], p['w_b_out'][j])
            b_c.append(cT)
            b_n.append(nT)
            b_m.append(mT)
        else:
            pkv, pkr = (None, None) if past is None else (past[5][j], past[6][j])
            out, kvr, krr = mla(h, pkv, pkr, p['w_c_dq'][j], p['g_c_q'][j], p['w_c_uq'][j], p['w_c_dkv'][j],
                                p['g_c_kv'][j], p['w_c_ukv'][j], p['w_c_o'][j])
            c_kv.append(kvr)
            c_kr.append(krr)
        x = x + gt1 * out
        h = rmsnorm(x, p['g_norm2'][i]) * (1.0 + sc2) + sh2
        x = x + gt2 * swiglu(h, p['w_ffn_in'][i], p['w_ffn_out'][i])
    y = rmsnorm(x, p['g_final'])
    return y, (jnp.stack(a_k), jnp.stack(a_v), jnp.stack(b_c), jnp.stack(b_n), jnp.stack(b_m),
               jnp.stack(c_kv), jnp.stack(c_kr))


def setup_inputs(seed: int = 0) -> dict:
    key = jax.random.key(seed)
    ks = iter(jax.random.split(key, 48))
    d = D_MODEL

    def nrm(shape, scale):
        return jax.random.normal(next(ks), shape, jnp.float32) * scale

    n_b_in = 2 * H_B * DQK_B + 2 * H_B * DV_B + 2 * H_B
    return {
        'x_prompt': nrm((BATCH, SEQ, d), 1.0),
        'x_sample': nrm((DEC_BATCH, DEC_SEQ, d), 1.0),
        'c_prompt': nrm((BATCH, d), 1.0),
        'c_sample': nrm((DEC_BATCH, d), 1.0),
        'cache_a_k': nrm((N_A_LAYERS, DEC_BATCH, PAST_LEN, H_A, 2 * D_SUB), 1.0),
        'cache_a_v': nrm((N_A_LAYERS, DEC_BATCH, PAST_LEN, H_A, 2 * D_SUB), 1.0),
        'state_b_c': nrm((N_B_LAYERS, DEC_BATCH, H_B, DQK_B, DV_B), 0.1),
        'state_b_n': nrm((N_B_LAYERS, DEC_BATCH, H_B, DQK_B), 0.1),
        'state_b_m': nrm((N_B_LAYERS, DEC_BATCH, H_B), 0.5),
        'cache_c_kv': nrm((N_C_LAYERS, DEC_BATCH, PAST_LEN, KV_LORA), 1.0),
        'cache_c_kr': nrm((N_C_LAYERS, DEC_BATCH, PAST_LEN, ROPE_C), 1.0),
        'w_ada': nrm((DEPTH, d, 6 * d), 0.5 * d ** -0.5),
        'b_ada': nrm((DEPTH, 6 * d), 0.02),
        'g_norm1': 1.0 + nrm((DEPTH, d), 0.02),
        'g_norm2': 1.0 + nrm((DEPTH, d), 0.02),
        'w_a_qkv': nrm((N_A_LAYERS, d, 3 * d), d ** -0.5),
        'a_lambda': nrm((N_A_LAYERS, 4, D_SUB), 0.1),
        'g_a_sub': 1.0 + nrm((N_A_LAYERS, 2 * D_SUB), 0.02),
        'w_a_o': nrm((N_A_LAYERS, d, d), d ** -0.5),
        'w_b_in': nrm((N_B_LAYERS, d, n_b_in), d ** -0.5),
        'b_b_gates': jnp.concatenate([nrm((N_B_LAYERS, H_B), 0.1),
                                      FORGET_BIAS + nrm((N_B_LAYERS, H_B), 0.1)], axis=-1),
        'g_b_out': 1.0 + nrm((N_B_LAYERS, H_B * DV_B), 0.02),
        'w_b_out': nrm((N_B_LAYERS, H_B * DV_B, d), (H_B * DV_B) ** -0.5),
        'w_c_dq': nrm((N_C_LAYERS, d, Q_LORA), d ** -0.5),
        'g_c_q': 1.0 + nrm((N_C_LAYERS, Q_LORA), 0.02),
        'w_c_uq': nrm((N_C_LAYERS, Q_LORA, H_C * (NOPE_C + ROPE_C)), Q_LORA ** -0.5),
        'w_c_dkv': nrm((N_C_LAYERS, d, KV_LORA + ROPE_C), d ** -0.5),
        'g_c_kv': 1.0 + nrm((N_C_LAYERS, KV_LORA), 0.02),
        'w_c_ukv': nrm((N_C_LAYERS, KV_LORA, H_C * (NOPE_C + V_C)), KV_LORA ** -0.5),
        'w_c_o': nrm((N_C_LAYERS, H_C * V_C, d), (H_C * V_C) ** -0.5),
        'w_ffn_in': nrm((DEPTH, d, 2 * D_FF), d ** -0.5),
        'w_ffn_out': nrm((DEPTH, D_FF, d), D_FF ** -0.5),
        'g_final': 1.0 + nrm((d,), 0.02),
    }


def reference(x_prompt, x_sample, c_prompt, c_sample, cache_a_k, cache_a_v, state_b_c, state_b_n,
              state_b_m, cache_c_kv, cache_c_kr, w_ada, b_ada, g_norm1, g_norm2, w_a_qkv, a_lambda,
              g_a_sub, w_a_o, w_b_in, b_b_gates, g_b_out, w_b_out, w_c_dq, g_c_q, w_c_uq, w_c_dkv,
              g_c_kv, w_c_ukv, w_c_o, w_ffn_in, w_ffn_out, g_final):
    p = dict(w_ada=w_ada, b_ada=b_ada, g_norm1=g_norm1, g_norm2=g_norm2, w_a_qkv=w_a_qkv,
             a_lambda=a_lambda, g_a_sub=g_a_sub, w_a_o=w_a_o, w_b_in=w_b_in, b_b_gates=b_b_gates,
             g_b_out=g_b_out, w_b_out=w_b_out, w_c_dq=w_c_dq, g_c_q=g_c_q, w_c_uq=w_c_uq,
             w_c_dkv=w_c_dkv, g_c_kv=g_c_kv, w_c_ukv=w_c_ukv, w_c_o=w_c_o, w_ffn_in=w_ffn_in,
             w_ffn_out=w_ffn_out, g_final=g_final)
    y_prompt, sp = trunk(x_prompt, c_prompt, None, p)
    past = (cache_a_k, cache_a_v, state_b_c, state_b_n, state_b_m, cache_c_kv, cache_c_kr)
    y_sample, ss = trunk(x_sample, c_sample, past, p)
    a_k_p, a_v_p, b_c_p, b_n_p, b_m_p, c_kv_p, c_kr_p = sp
    a_k_s, a_v_s, b_c_s, b_n_s, b_m_s, c_kv_s, c_kr_s = ss
    return (y_prompt, y_sample, a_k_p, a_v_p, b_c_p, b_n_p, b_m_p, c_kv_p, c_kr_p,
            a_k_s, a_v_s, b_c_s, b_n_s, b_m_s, c_kv_s, c_kr_s)
```

```python
import functools
import math

import jax
import jax.numpy as jnp
from jax import lax
from jax.experimental import pallas as pl
from jax.experimental.pallas import tpu as pltpu

F32 = jnp.float32
BF16 = jnp.bfloat16

D_MODEL = 1024
DEPTH = 4
CHUNK = 64
N_MIXERS = 3
ROPE_THETA = 500000.0
EPS = 1e-6
D_SUB = 64
H_A = D_MODEL // (2 * D_SUB)
ROT_A = D_SUB // 4
H_B = 4
DQK_B = D_MODEL // (2 * H_B)
DV_B = D_MODEL // H_B
H_C = 8
NOPE_C = 128
ROPE_C = 64
V_C = 128
Q_LORA = 384
KV_LORA = 256
D_FF = -(-8 * D_MODEL // (3 * 256)) * 256

LANES = 128
DQK_C_PAD = 2 * LANES
NEG = -1e30
VMEM_LIMIT = 56 * 1024 * 1024
TM = 512
TQ = 512
FF_CHUNK = 256


def _params(sem):
    return pltpu.CompilerParams(dimension_semantics=sem, vmem_limit_bytes=VMEM_LIMIT)


def _const(shape):
    nd = len(shape)
    return pl.BlockSpec(shape, lambda *_: (0,) * nd, pipeline_mode=pl.Buffered(1))


def _rms(x, g):
    return x * lax.rsqrt(jnp.mean(x * x, axis=-1, keepdims=True) + EPS) * g


def _dot(a, b):
    return jnp.dot(a, b, preferred_element_type=F32)


def _dot_nt(a, b):
    return lax.dot_general(a, b, (((1,), (1,)), ((), ())), preferred_element_type=F32)


def _rope_cols(c, cos, sin_lo, sin_hi, half):
    return c * cos + pltpu.roll(c, LANES - half, 1) * sin_lo + pltpu.roll(c, half, 1) * sin_hi


def _ada_kernel(c_ref, w_ref, b_ref, o_ref):
    c = c_ref[...]
    s = (c * jax.nn.sigmoid(c)).astype(BF16)
    o_ref[0] = _dot(s, w_ref[0].astype(BF16)) + b_ref[0]


def _ada(c_all, w_ada, b_ada):
    n = c_all.shape[0]
    tn = 1536
    return pl.pallas_call(
        _ada_kernel,
        grid=(DEPTH, 6 * D_MODEL // tn),
        in_specs=[pl.BlockSpec((n, D_MODEL), lambda i, j: (0, 0)),
                  pl.BlockSpec((1, D_MODEL, tn), lambda i, j: (i, 0, j)),
                  pl.BlockSpec((1, 1, tn), lambda i, j: (i, 0, j))],
        out_specs=pl.BlockSpec((1, n, tn), lambda i, j: (i, 0, j)),
        out_shape=jax.ShapeDtypeStruct((DEPTH, n, 6 * D_MODEL), F32),
        compiler_params=_params(("arbitrary", "arbitrary")),
        name="ada",
    )(c_all, w_ada, b_ada.reshape(DEPTH, 1, 6 * D_MODEL))


class _Path:
    def __init__(self, batch, seq):
        self.batch, self.seq = batch, seq
        self.n = batch * seq
        if seq >= TM:
            self.tm, self.tpb, self.mod_rows = TM, seq // TM, 1
        else:
            self.tm, self.tpb, self.mod_rows = self.n, 1, self.n
        self.steps = self.n // self.tm
        self.table_rows = self.tpb * self.tm

    def rows(self, width):
        return pl.BlockSpec((self.tm, width), lambda i: (i, 0))

    def mod(self):
        tpb = self.tpb
        return pl.BlockSpec((1, self.mod_rows, D_MODEL), lambda i: (i // tpb, 0, 0))

    def table(self):
        tpb = self.tpb
        return pl.BlockSpec((self.tm, LANES), lambda i: (i % tpb, 0))

    def expand_mod(self, m):
        if self.mod_rows == 1:
            return m[:, None, :]
        return jnp.repeat(m, self.seq, axis=0)[None]

    def expand_table(self, t):
        if self.table_rows == self.seq:
            return t
        return jnp.tile(t, (self.table_rows // self.seq, 1))


def _rope_tables(pos, rot, period):
    half = rot // 2
    inv = jnp.power(jnp.float32(ROPE_THETA), -jnp.arange(half, dtype=jnp.float32) * (2.0 / rot))
    ang = pos.astype(jnp.float32)[:, None] * inv[None, :]
    cos, sin = jnp.cos(ang), jnp.sin(ang)
    n = pos.shape[0]
    reps = LANES // period
    pad1 = jnp.ones((n, period - rot), F32)
    pad0 = jnp.zeros((n, period - rot), F32)
    zh = jnp.zeros((n, half), F32)
    cos_t = jnp.tile(jnp.concatenate([cos, cos, pad1], axis=1), (1, reps))
    sin_lo = jnp.tile(jnp.concatenate([-sin, zh, pad0], axis=1), (1, reps))
    sin_hi = jnp.tile(jnp.concatenate([zh, sin, pad0], axis=1), (1, reps))
    return cos_t, sin_lo, sin_hi


def _proj_a_kernel(x_ref, sh_ref, sc_ref, g_ref, w_ref, cos_ref, slo_ref, shi_ref,
                   q_ref, k_ref, v_ref, krow_ref, vrow_ref):
    h = (_rms(x_ref[...], g_ref[...]) * (1.0 + sc_ref[0]) + sh_ref[0]).astype(BF16)
    cos, slo, shi = cos_ref[...], slo_ref[...], shi_ref[...]
    q = _dot(h, w_ref[:, 0:D_MODEL])
    k = _dot(h, w_ref[:, D_MODEL:2 * D_MODEL])
    for j in range(D_MODEL // LANES):
        cs = slice(j * LANES, (j + 1) * LANES)
        qj = _rope_cols(q[:, cs], cos, slo, shi, ROT_A // 2)
        kj = _rope_cols(k[:, cs], cos, slo, shi, ROT_A // 2)
        q_ref[:, cs] = (qj * (D_SUB ** -0.5)).astype(BF16)
        krow_ref[:, cs] = kj
        k_ref[:, cs] = kj.astype(BF16)
    v = _dot(h, w_ref[:, 2 * D_MODEL:3 * D_MODEL])
    vrow_ref[...] = v
    v_ref[...] = v.astype(BF16)


def _proj_a(path, x, sh, sc, g, w_qkv, tables):
    n = path.n
    bf = jax.ShapeDtypeStruct((n, D_MODEL), BF16)
    f32 = jax.ShapeDtypeStruct((n, D_MODEL), F32)
    return pl.pallas_call(
        _proj_a_kernel,
        grid=(path.steps,),
        in_specs=[path.rows(D_MODEL), path.mod(), path.mod(), _const((1, D_MODEL)),
                  _const((D_MODEL, 3 * D_MODEL)), path.table(), path.table(), path.table()],
        out_specs=[path.rows(D_MODEL)] * 5,
        out_shape=[bf, bf, bf, f32, f32],
        compiler_params=_params(("arbitrary",)),
        name="proj_a",
    )(x, sh, sc, g, w_qkv, *tables)


def _lambda_full(lam_ref, lambda_init):
    lv = lam_ref[0]
    d1 = jnp.sum(lv[0:1] * lv[1:2], axis=-1, keepdims=True)
    d2 = jnp.sum(lv[2:3] * lv[3:4], axis=-1, keepdims=True)
    return jnp.exp(d1) - jnp.exp(d2) + lambda_init


def _split_subheads(q):
    lane = lax.broadcasted_iota(jnp.int32, q.shape, 1)
    zero = jnp.zeros_like(q)
    return [jnp.where(lane < D_SUB, q, zero), jnp.where(lane >= D_SUB, q, zero)]


def _diff_combine(o1, o2, lam, g_sub, lambda_init):
    o = o1 - lam * o2
    return _rms(o, g_sub) * (1.0 - lambda_init)


def _flash_kernel(*refs, diff, lambda_init, tq):
    if diff:
        q_ref, k_ref, v_ref, lam_ref, gsub_ref, o_ref, m_scr, l_scr, acc_scr = refs
    else:
        q_ref, k_ref, v_ref, o_ref, m_scr, l_scr, acc_scr = refs
    qi = pl.program_id(2)
    q = q_ref[0]
    qs = _split_subheads(q) if diff else [q]
    n_sub = len(qs)
    m_scr[...] = jnp.full(m_scr.shape, NEG, F32)
    l_scr[...] = jnp.zeros(l_scr.shape, F32)
    acc_scr[...] = jnp.zeros(acc_scr.shape, F32)
    row = lax.broadcasted_iota(jnp.int32, (tq, tq), 0)
    col = lax.broadcasted_iota(jnp.int32, (tq, tq), 1)
    visible = (col // CHUNK) <= (row // CHUNK)

    def block(j, masked):
        start = pl.multiple_of(j * tq, tq)
        kb = k_ref[0, pl.ds(start, tq), :]
        vb = v_ref[0, pl.ds(start, tq), :]
        for s in range(n_sub):
            sc = _dot_nt(qs[s], kb)
            if masked:
                sc = jnp.where(visible, sc, NEG)
            m_prev = m_scr[s]
            m_new = jnp.maximum(m_prev, jnp.max(sc, axis=-1, keepdims=True))
            alpha = jnp.exp(m_prev - m_new)
            p = jnp.exp(sc - m_new)
            l_scr[s] = alpha * l_scr[s] + jnp.sum(p, axis=-1, keepdims=True)
            acc_scr[s] = alpha * acc_scr[s] + _dot(p.astype(BF16), vb)
            m_scr[s] = m_new

    def body(j, carry):
        block(j, False)
        return carry

    lax.fori_loop(0, qi, body, 0)
    block(qi, True)
    if diff:
        o = _diff_combine(acc_scr[0] / l_scr[0], acc_scr[1] / l_scr[1],
                          _lambda_full(lam_ref, lambda_init), gsub_ref[0], lambda_init)
    else:
        o = acc_scr[0] / l_scr[0]
    o_ref[0] = o.astype(BF16)


def _flash(q, k, v, batch, seq, heads, dqk, diff_args=None, lambda_init=0.0):
    diff = diff_args is not None
    tq = TQ
    n_sub = 2 if diff else 1
    in_specs = [pl.BlockSpec((1, tq, dqk), lambda b, h, i: (b, i, h)),
                pl.BlockSpec((1, seq, dqk), lambda b, h, i: (b, 0, h)),
                pl.BlockSpec((1, seq, LANES), lambda b, h, i: (b, 0, h))]
    args = [q, k, v]
    if diff:
        lam, g_sub, jl = diff_args
        in_specs += [pl.BlockSpec((1, 4, D_SUB), lambda b, h, i: (jl, 0, 0)),
                     pl.BlockSpec((1, 1, LANES), lambda b, h, i: (jl, 0, 0))]
        args += [lam, g_sub]
    return pl.pallas_call(
        functools.partial(_flash_kernel, diff=diff, lambda_init=lambda_init, tq=tq),
        grid=(batch, heads, seq // tq),
        in_specs=in_specs,
        out_specs=pl.BlockSpec((1, tq, LANES), lambda b, h, i: (b, i, h)),
        out_shape=jax.ShapeDtypeStruct((batch, seq, heads * LANES), BF16),
        scratch_shapes=[pltpu.VMEM((n_sub, tq, 1), F32), pltpu.VMEM((n_sub, tq, 1), F32),
                        pltpu.VMEM((n_sub, tq, LANES), F32)],
        compiler_params=_params(("arbitrary", "arbitrary", "arbitrary")),
        name="flash_diff" if diff else "flash_mla",
    )(*args)


def _softmax_two(s_past, s_new):
    m = jnp.maximum(jnp.max(s_past, axis=-1, keepdims=True), jnp.max(s_new, axis=-1, keepdims=True))
    p_past = jnp.exp(s_past - m)
    p_new = jnp.exp(s_new - m)
    l = jnp.sum(p_past, axis=-1, keepdims=True) + jnp.sum(p_new, axis=-1, keepdims=True)
    return p_past, p_new, l


def _decode_a_kernel(q_ref, kn_ref, vn_ref, kp_ref, vp_ref, lam_ref, gsub_ref, o_ref, *, lambda_init):
    qs = _split_subheads(q_ref[0])
    kp = kp_ref[0, 0].astype(BF16)
    vp = vp_ref[0, 0].astype(BF16)
    kn, vn = kn_ref[0], vn_ref[0]
    outs = []
    for qsub in qs:
        p_past, p_new, l = _softmax_two(_dot_nt(qsub, kp), _dot_nt(qsub, kn))
        outs.append((_dot(p_past.astype(BF16), vp) + _dot(p_new.astype(BF16), vn)) / l)
    o = _diff_combine(outs[0], outs[1], _lambda_full(lam_ref, lambda_init), gsub_ref[0], lambda_init)
    o_ref[0] = o.astype(BF16)


def _decode_a(q, k, v, cache_k, cache_v, lam, g_sub, jl, lambda_init, batch, seq):
    past = cache_k.shape[2]
    new = lambda b, h: (b, 0, h)
    old = lambda b, h: (jl, b, 0, h)
    return pl.pallas_call(
        functools.partial(_decode_a_kernel, lambda_init=lambda_init),
        grid=(batch, H_A),
        in_specs=[pl.BlockSpec((1, seq, LANES), new), pl.BlockSpec((1, seq, LANES), new),
                  pl.BlockSpec((1, seq, LANES), new),
                  pl.BlockSpec((1, 1, past, LANES), old), pl.BlockSpec((1, 1, past, LANES), old),
                  pl.BlockSpec((1, 4, D_SUB), lambda b, h: (jl, 0, 0)),
                  pl.BlockSpec((1, 1, LANES), lambda b, h: (jl, 0, 0))],
        out_specs=pl.BlockSpec((1, seq, LANES), new),
        out_shape=jax.ShapeDtypeStruct((batch, seq, D_MODEL), BF16),
        compiler_params=_params(("arbitrary", "arbitrary")),
        name="decode_a",
    )(q, k, v, cache_k, cache_v, lam, g_sub)


def _proj_c_kernel(x_ref, sh_ref, sc_ref, g_ref, w1_ref, gq_ref, wuq_ref, gkv_ref, wukv_ref,
                   cos_ref, slo_ref, shi_ref, q_ref, k_ref, v_ref, kvrow_ref, krrow_ref):
    h = (_rms(x_ref[...], g_ref[...]) * (1.0 + sc_ref[0]) + sh_ref[0]).astype(BF16)
    cos, slo, shi = cos_ref[...], slo_ref[...], shi_ref[...]
    d = _dot(h, w1_ref[...])
    qlat = _rms(d[:, 0:Q_LORA], gq_ref[...]).astype(BF16)
    kv_row = _rms(d[:, Q_LORA:Q_LORA + KV_LORA], gkv_ref[...])
    kr = _rope_cols(d[:, Q_LORA + KV_LORA:], cos, slo, shi, ROPE_C // 2)
    kvrow_ref[...] = kv_row
    krrow_ref[...] = kr[:, 0:ROPE_C]
    kr_bf = kr.astype(BF16)
    q = _dot(qlat, wuq_ref[...]) * ((NOPE_C + ROPE_C) ** -0.5)
    kv = _dot(kv_row.astype(BF16), wukv_ref[...])
    for hh in range(H_C):
        lo = slice(hh * DQK_C_PAD, hh * DQK_C_PAD + LANES)
        hi = slice(hh * DQK_C_PAD + LANES, (hh + 1) * DQK_C_PAD)
        q_ref[:, lo] = q[:, lo].astype(BF16)
        q_ref[:, hi] = _rope_cols(q[:, hi], cos, slo, shi, ROPE_C // 2).astype(BF16)
        k_ref[:, lo] = kv[:, lo].astype(BF16)
        k_ref[:, hi] = kr_bf
        v_ref[:, hh * LANES:(hh + 1) * LANES] = kv[:, hi].astype(BF16)


def _proj_c(path, x, sh, sc, g, w1, g_q, w_uq, g_kv, w_ukv, tables):
    n = path.n
    w1n = w1.shape[1]
    return pl.pallas_call(
        _proj_c_kernel,
        grid=(path.steps,),
        in_specs=[path.rows(D_MODEL), path.mod(), path.mod(), _const((1, D_MODEL)),
                  _const((D_MODEL, w1n)), _const((1, Q_LORA)), _const((Q_LORA, H_C * DQK_C_PAD)),
                  _const((1, KV_LORA)), _const((KV_LORA, H_C * (NOPE_C + V_C))),
                  path.table(), path.table(), path.table()],
        out_specs=[path.rows(H_C * DQK_C_PAD), path.rows(H_C * DQK_C_PAD), path.rows(H_C * V_C),
                   path.rows(KV_LORA), path.rows(ROPE_C)],
        out_shape=[jax.ShapeDtypeStruct((n, H_C * DQK_C_PAD), BF16),
                   jax.ShapeDtypeStruct((n, H_C * DQK_C_PAD), BF16),
                   jax.ShapeDtypeStruct((n, H_C * V_C), BF16),
                   jax.ShapeDtypeStruct((n, KV_LORA), F32),
                   jax.ShapeDtypeStruct((n, ROPE_C), F32)],
        compiler_params=_params(("arbitrary",)),
        name="proj_c",
    )(x, sh, sc, g, w1, g_q, w_uq, g_kv, w_ukv, *tables)


def _decode_c_kernel(q_ref, kn_ref, vn_ref, kvp_ref, krp_ref, wukv_ref, o_ref):
    q = q_ref[0]
    kvh = _dot(kvp_ref[0, 0].astype(BF16), wukv_ref[...])
    k_past = kvh[:, 0:NOPE_C].astype(BF16)
    v_past = kvh[:, NOPE_C:].astype(BF16)
    kr_past = krp_ref[0, 0].astype(BF16)
    s_past = _dot_nt(q[:, 0:NOPE_C], k_past) + _dot_nt(q[:, NOPE_C:NOPE_C + ROPE_C], kr_past)
    p_past, p_new, l = _softmax_two(s_past, _dot_nt(q, kn_ref[0]))
    o = (_dot(p_past.astype(BF16), v_past) + _dot(p_new.astype(BF16), vn_ref[0])) / l
    o_ref[0] = o.astype(BF16)


def _decode_c(q, k, v, cache_kv, cache_kr, w_ukv, jl, batch, seq):
    past = cache_kv.shape[2]
    return pl.pallas_call(
        _decode_c_kernel,
        grid=(batch, H_C),
        in_specs=[pl.BlockSpec((1, seq, DQK_C_PAD), lambda b, h: (b, 0, h)),
                  pl.BlockSpec((1, seq, DQK_C_PAD), lambda b, h: (b, 0, h)),
                  pl.BlockSpec((1, seq, LANES), lambda b, h: (b, 0, h)),
                  pl.BlockSpec((1, 1, past, KV_LORA), lambda b, h: (jl, b, 0, 0)),
                  pl.BlockSpec((1, 1, past, ROPE_C), lambda b, h: (jl, b, 0, 0)),
                  pl.BlockSpec((KV_LORA, NOPE_C + V_C), lambda b, h: (0, h))],
        out_specs=pl.BlockSpec((1, seq, LANES), lambda b, h: (b, 0, h)),
        out_shape=jax.ShapeDtypeStruct((batch, seq, H_C * V_C), BF16),
        compiler_params=_params(("arbitrary", "arbitrary")),
        name="decode_c",
    )(q, k, v, cache_kv, cache_kr, w_ukv)


def _proj_b_kernel(x_ref, sh_ref, sc_ref, g_ref, w_ref, bg_ref,
                   q_ref, k_ref, kt_ref, v_ref, opre_ref, gates_ref, gt_ref):
    nqk, nv = H_B * DQK_B, H_B * DV_B
    h = (_rms(x_ref[...], g_ref[...]) * (1.0 + sc_ref[0]) + sh_ref[0]).astype(BF16)
    q_ref[...] = _dot(h, w_ref[:, 0:nqk]).astype(BF16)
    k = _dot(h, w_ref[:, nqk:2 * nqk]) * (DQK_B ** -0.5)
    k_ref[...] = k
    kt_ref[0] = k.T
    v_ref[...] = _dot(h, w_ref[:, 2 * nqk:2 * nqk + nv]).astype(BF16)
    opre_ref[...] = _dot(h, w_ref[:, 2 * nqk + nv:2 * nqk + 2 * nv])
    gates = _dot(h, w_ref[:, 2 * nqk + 2 * nv:]) + bg_ref[...]
    gates_ref[...] = gates
    gt_ref[0] = gates.T[0:2 * H_B, :]


def _proj_b(path, x, sh, sc, g, w_in, b_gates):
    n, nqk, nv = path.n, H_B * DQK_B, H_B * DV_B
    groups = path.n // path.table_rows
    tpb = path.tpb
    cols = lambda width: pl.BlockSpec((1, width, path.tm), lambda i: (i // tpb, 0, i % tpb))
    return pl.pallas_call(
        _proj_b_kernel,
        grid=(path.steps,),
        in_specs=[path.rows(D_MODEL), path.mod(), path.mod(), _const((1, D_MODEL)),
                  _const((D_MODEL, w_in.shape[1])), _const((1, LANES))],
        out_specs=[path.rows(nqk), path.rows(nqk), cols(nqk), path.rows(nv), path.rows(nv),
                   path.rows(LANES), cols(2 * H_B)],
        out_shape=[jax.ShapeDtypeStruct((n, nqk), BF16), jax.ShapeDtypeStruct((n, nqk), F32),
                   jax.ShapeDtypeStruct((groups, nqk, path.table_rows), F32),
                   jax.ShapeDtypeStruct((n, nv), BF16), jax.ShapeDtypeStruct((n, nv), F32),
                   jax.ShapeDtypeStruct((n, LANES), F32),
                   jax.ShapeDtypeStruct((groups, 2 * H_B, path.table_rows), F32)],
        compiler_params=_params(("arbitrary",)),
        name="proj_b",
    )(x, sh, sc, g, w_in, b_gates)


def _mlstm_kernel(q_ref, k_ref, kt_ref, v_ref, g_ref, gt_ref, c0_ref, n0_ref, m0_ref,
                  hs_ref, ct_ref, nt_ref, mt_ref, c_scr, n_scr, m_scr, *, chunk, chunks_per_step):
    step = pl.program_id(1)

    @pl.when(step == 0)
    def _():
        c_scr[...] = c0_ref[0]
        n_scr[...] = n0_ref[0]
        m_scr[...] = m0_ref[0]

    L = chunk
    row = lax.broadcasted_iota(jnp.int32, (L, L), 0)
    col = lax.broadcasted_iota(jnp.int32, (L, L), 1)
    tril = col <= row
    triu = row <= col
    for c in range(chunks_per_step):
        sl = slice(c * L, (c + 1) * L)
        g_col = g_ref[0, sl, :]
        g_row = gt_ref[0, :, sl]
        lf_col = jax.nn.log_sigmoid(g_col)
        lf_row = jax.nn.log_sigmoid(g_row)
        for h in range(H_B):
            ig_col, f_col = g_col[:, h:h + 1], lf_col[:, H_B + h:H_B + h + 1]
            ig_row, f_row = g_row[h:h + 1, :], lf_row[H_B + h:H_B + h + 1, :]
            b_col = jnp.sum(jnp.where(tril, f_row, 0.0), axis=1, keepdims=True)
            b_row = jnp.sum(jnp.where(triu, f_col, 0.0), axis=0, keepdims=True)
            g_tot = jnp.sum(f_row, axis=1, keepdims=True)
            m_prev = m_scr[h]
            dm = jnp.where(tril, b_col - b_row + ig_row, NEG)
            inter = b_col + m_prev
            m_t = jnp.maximum(inter, jnp.max(dm, axis=1, keepdims=True))
            w_ts = jnp.exp(dm - m_t)
            a = jnp.exp(inter - m_t)
            qh = q_ref[0, sl, h * DQK_B:(h + 1) * DQK_B]
            kh = k_ref[0, sl, h * DQK_B:(h + 1) * DQK_B]
            vh = v_ref[0, sl, h * DV_B:(h + 1) * DV_B]
            c_prev = c_scr[h]
            n_prev = n_scr[h]
            qk = _dot_nt(qh, kh.astype(BF16)) * w_ts
            num = a * _dot(qh, c_prev.astype(BF16)) + _dot(qk.astype(BF16), vh)
            qn = jnp.sum(qh.astype(F32) * n_prev, axis=1, keepdims=True)
            den = a * qn + jnp.sum(qk, axis=1, keepdims=True)
            hs_ref[0, sl, h * DV_B:(h + 1) * DV_B] = num / jnp.maximum(jnp.abs(den), jnp.exp(-m_t))
            r_row = g_tot - b_row + ig_row
            r_col = g_tot - b_col + ig_col
            m_new = jnp.maximum(g_tot + m_prev, jnp.max(r_row, axis=1, keepdims=True))
            decay = jnp.exp(g_tot + m_prev - m_new)
            kw_t = kt_ref[0, h * DQK_B:(h + 1) * DQK_B, sl] * jnp.exp(r_row - m_new)
            c_scr[h] = decay * c_prev + _dot(kw_t.astype(BF16), vh)
            n_scr[h] = decay * n_prev + jnp.sum(kh * jnp.exp(r_col - m_new), axis=0, keepdims=True)
            m_scr[h] = m_new

    @pl.when(step == pl.num_programs(1) - 1)
    def _():
        ct_ref[0] = c_scr[...]
        nt_ref[0] = n_scr[...]
        mt_ref[0] = m_scr[...]


def _mlstm(q, k, kt, v, gates, gt, c0, n0, m0, batch, seq):
    chunk = CHUNK if seq % CHUNK == 0 else seq
    cps = max(1, min(LANES // chunk, seq // chunk))
    span = chunk * cps
    nqk, nv = H_B * DQK_B, H_B * DV_B
    rows = lambda width: pl.BlockSpec((1, span, width), lambda b, s: (b, s, 0))
    cols = lambda height: pl.BlockSpec((1, height, span), lambda b, s: (b, 0, s))
    state = lambda *shape: pl.BlockSpec((1,) + shape, lambda b, s: (b,) + (0,) * len(shape))
    return pl.pallas_call(
        functools.partial(_mlstm_kernel, chunk=chunk, chunks_per_step=cps),
        grid=(batch, seq // span),
        in_specs=[rows(nqk), rows(nqk), cols(nqk), rows(nv), rows(LANES), cols(2 * H_B),
                  state(H_B, DQK_B, DV_B), state(H_B, 1, DQK_B), state(H_B, 1, 1)],
        out_specs=[rows(nv), state(H_B, DQK_B, DV_B), state(H_B, 1, DQK_B), state(H_B, 1, 1)],
        out_shape=[jax.ShapeDtypeStruct((batch, seq, nv), F32),
                   jax.ShapeDtypeStruct((batch, H_B, DQK_B, DV_B), F32),
                   jax.ShapeDtypeStruct((batch, H_B, 1, DQK_B), F32),
                   jax.ShapeDtypeStruct((batch, H_B, 1, 1), F32)],
        scratch_shapes=[pltpu.VMEM((H_B, DQK_B, DV_B), F32), pltpu.VMEM((H_B, 1, DQK_B), F32),
                        pltpu.VMEM((H_B, 1, 1), F32)],
        compiler_params=_params(("arbitrary", "arbitrary")),
        name="mlstm",
    )(q, k, kt, v, gates, gt, c0, n0, m0)


def _out_kernel(o_ref, x_ref, gt_ref, w_ref, y_ref):
    y_ref[...] = x_ref[...] + gt_ref[0] * _dot(o_ref[...], w_ref[...])


def _out_b_kernel(hs_ref, opre_ref, gout_ref, x_ref, gt_ref, w_ref, y_ref):
    hs = hs_ref[...]
    gout = gout_ref[...]
    cols = []
    for h in range(H_B):
        cs = slice(h * DV_B, (h + 1) * DV_B)
        cols.append(_rms(hs[:, cs], gout[:, cs]))
    z = (jax.nn.sigmoid(opre_ref[...]) * jnp.concatenate(cols, axis=-1)).astype(BF16)
    y_ref[...] = x_ref[...] + gt_ref[0] * _dot(z, w_ref[...])


def _out_proj(path, o, x, gt, w):
    return pl.pallas_call(
        _out_kernel,
        grid=(path.steps,),
        in_specs=[path.rows(D_MODEL), path.rows(D_MODEL), path.mod(), _const((D_MODEL, D_MODEL))],
        out_specs=path.rows(D_MODEL),
        out_shape=jax.ShapeDtypeStruct((path.n, D_MODEL), F32),
        compiler_params=_params(("arbitrary",)),
        name="out_proj",
    )(o, x, gt, w)


def _out_proj_b(path, hs, opre, g_out, x, gt, w):
    return pl.pallas_call(
        _out_b_kernel,
        grid=(path.steps,),
        in_specs=[path.rows(D_MODEL), path.rows(D_MODEL), _const((1, D_MODEL)), path.rows(D_MODEL),
                  path.mod(), _const((D_MODEL, D_MODEL))],
        out_specs=path.rows(D_MODEL),
        out_shape=jax.ShapeDtypeStruct((path.n, D_MODEL), F32),
        compiler_params=_params(("arbitrary",)),
        name="out_proj_b",
    )(hs, opre, g_out, x, gt, w)


def _ffn_kernel(x_ref, sh_ref, sc_ref, gt_ref, g_ref, win_ref, wout_ref, gfin_ref, y_ref, acc_scr, *, final):
    x = x_ref[...]
    h = (_rms(x, g_ref[...]) * (1.0 + sc_ref[0]) + sh_ref[0]).astype(BF16)
    for c in range(D_FF // FF_CHUNK):
        a = _dot(h, win_ref[:, c * FF_CHUNK:(c + 1) * FF_CHUNK])
        b = _dot(h, win_ref[:, D_FF + c * FF_CHUNK:D_FF + (c + 1) * FF_CHUNK])
        act = (a * jax.nn.sigmoid(a) * b).astype(BF16)
        part = _dot(act, wout_ref[c * FF_CHUNK:(c + 1) * FF_CHUNK, :])
        if c == 0:
            acc_scr[...] = part
        else:
            acc_scr[...] += part
    y = x + gt_ref[0] * acc_scr[...]
    y_ref[...] = _rms(y, gfin_ref[...]) if final else y


def _ffn(path, x, sh, sc, gt, g, w_in, w_out, g_final, final):
    return pl.pallas_call(
        functools.partial(_ffn_kernel, final=final),
        grid=(path.steps,),
        in_specs=[path.rows(D_MODEL), path.mod(), path.mod(), path.mod(), _const((1, D_MODEL)),
                  _const((D_MODEL, 2 * D_FF)), _const((D_FF, D_MODEL)), _const((1, D_MODEL))],
        out_specs=path.rows(D_MODEL),
        out_shape=jax.ShapeDtypeStruct((path.n, D_MODEL), F32),
        scratch_shapes=[pltpu.VMEM((path.tm, D_MODEL), F32)],
        compiler_params=_params(("arbitrary",)),
        name="ffn",
    )(x, sh, sc, gt, g, w_in, w_out, g_final)


def _prep_weights(p):
    w = {}
    w['a_qkv'] = p['w_a_qkv'].astype(BF16)
    w['a_o'] = p['w_a_o'].astype(BF16)
    nqk, nv = H_B * DQK_B, H_B * DV_B
    w_b = p['w_b_in']
    gate_pad = jnp.zeros(w_b.shape[:2] + (LANES - 2 * H_B,), w_b.dtype)
    w['b_in'] = jnp.concatenate([w_b, gate_pad], axis=-1).astype(BF16)
    bg = p['b_b_gates']
    w['b_gates'] = jnp.concatenate([bg, jnp.zeros((bg.shape[0], LANES - 2 * H_B), bg.dtype)], axis=-1)[:, None, :]
    w['b_out'] = p['w_b_out'].astype(BF16)
    n_c = p['w_c_dq'].shape[0]
    w_dkv = p['w_c_dkv']
    w['c_1'] = jnp.concatenate([p['w_c_dq'], w_dkv, jnp.zeros((n_c, D_MODEL, LANES - ROPE_C), w_dkv.dtype)],
                               axis=-1).astype(BF16)
    w_uq = p['w_c_uq'].reshape(n_c, Q_LORA, H_C, NOPE_C + ROPE_C)
    w_uq = jnp.concatenate([w_uq, jnp.zeros((n_c, Q_LORA, H_C, DQK_C_PAD - NOPE_C - ROPE_C), w_uq.dtype)], axis=-1)
    w['c_uq'] = w_uq.reshape(n_c, Q_LORA, H_C * DQK_C_PAD).astype(BF16)
    w['c_ukv'] = p['w_c_ukv'].astype(BF16)
    w['c_o'] = p['w_c_o'].astype(BF16)
    w['ffn_in'] = p['w_ffn_in'].astype(BF16)
    w['ffn_out'] = p['w_ffn_out'].astype(BF16)
    return w


def _trunk(x, mods, past, p, w):
    batch, seq, _ = x.shape
    path = _Path(batch, seq)
    n_past = 0 if past is None else past[0].shape[2]
    pos = n_past + jnp.arange(seq, dtype=jnp.int32)
    tab_a = tuple(path.expand_table(t) for t in _rope_tables(pos, ROT_A, D_SUB))
    tab_c = tuple(path.expand_table(t) for t in _rope_tables(pos, ROPE_C, LANES))
    x = x.reshape(path.n, D_MODEL)
    g_final = p['g_final'][None, :]
    a_k, a_v, b_c, b_n, b_m, c_kv, c_kr = [], [], [], [], [], [], []
    for i in range(DEPTH):
        kind, j = i % N_MIXERS, i // N_MIXERS
        sh1, sc1, gt1, sh2, sc2, gt2 = (path.expand_mod(m) for m in jnp.split(mods[i], 6, axis=-1))
        g1 = p['g_norm1'][i][None, :]
        if kind == 0:
            lambda_init = 0.8 - 0.6 * math.exp(-0.3 * i)
            q, k, v, k_row, v_row = _proj_a(path, x, sh1, sc1, g1, w['a_qkv'][j], tab_a)
            shape3 = (batch, seq, D_MODEL)
            g_sub = p['g_a_sub'][:, None, :]
            if past is None:
                o = _flash(q.reshape(shape3), k.reshape(shape3), v.reshape(shape3), batch, seq, H_A, LANES,
                           diff_args=(p['a_lambda'], g_sub, j), lambda_init=lambda_init)
            else:
                o = _decode_a(q.reshape(shape3), k.reshape(shape3), v.reshape(shape3), past[0], past[1],
                              p['a_lambda'], g_sub, j, lambda_init, batch, seq)
            a_k.append(k_row.reshape(batch, seq, H_A, 2 * D_SUB))
            a_v.append(v_row.reshape(batch, seq, H_A, 2 * D_SUB))
            x = _out_proj(path, o.reshape(path.n, D_MODEL), x, gt1, w['a_o'][j])
        elif kind == 1:
            q, k, kt, v, opre, gates, gt = _proj_b(path, x, sh1, sc1, g1, w['b_in'][j], w['b_gates'][j])
            nqk, nv = H_B * DQK_B, H_B * DV_B
            if path.table_rows != seq:
                kt = kt.reshape(nqk, batch, seq).transpose(1, 0, 2)
                gt = gt.reshape(2 * H_B, batch, seq).transpose(1, 0, 2)
            if past is None:
                c0 = jnp.zeros((batch, H_B, DQK_B, DV_B), F32)
                n0 = jnp.zeros((batch, H_B, 1, DQK_B), F32)
                m0 = jnp.zeros((batch, H_B, 1, 1), F32)
            else:
                c0 = past[2][j]
                n0 = past[3][j][:, :, None, :]
                m0 = past[4][j][:, :, None, None]
            hs, c_t, n_t, m_t = _mlstm(q.reshape(batch, seq, nqk), k.reshape(batch, seq, nqk), kt,
                                       v.reshape(batch, seq, nv), gates.reshape(batch, seq, LANES), gt,
                                       c0, n0, m0, batch, seq)
            b_c.append(c_t)
            b_n.append(n_t[:, :, 0, :])
            b_m.append(m_t[:, :, 0, 0])
            x = _out_proj_b(path, hs.reshape(path.n, nv), opre, p['g_b_out'][j][None, :], x, gt1, w['b_out'][j])
        else:
            q, k, v, kv_row, kr_row = _proj_c(path, x, sh1, sc1, g1, w['c_1'][j], p['g_c_q'][j][None, :],
                                              w['c_uq'][j], p['g_c_kv'][j][None, :], w['c_ukv'][j], tab_c)
            qk_shape = (batch, seq, H_C * DQK_C_PAD)
            v_shape = (batch, seq, H_C * V_C)
            if past is None:
                o = _flash(q.reshape(qk_shape), k.reshape(qk_shape), v.reshape(v_shape), batch, seq, H_C, DQK_C_PAD)
            else:
                o = _decode_c(q.reshape(qk_shape), k.reshape(qk_shape), v.reshape(v_shape), past[5], past[6],
                              w['c_ukv'][j], j, batch, seq)
            c_kv.append(kv_row.reshape(batch, seq, KV_LORA))
            c_kr.append(kr_row.reshape(batch, seq, ROPE_C))
            x = _out_proj(path, o.reshape(path.n, D_MODEL), x, gt1, w['c_o'][j])
        x = _ffn(path, x, sh2, sc2, gt2, p['g_norm2'][i][None, :], w['ffn_in'][i], w['ffn_out'][i],
                 g_final, final=(i == DEPTH - 1))
    y = x.reshape(batch, seq, D_MODEL)
    return y, (jnp.stack(a_k), jnp.stack(a_v), jnp.stack(b_c), jnp.stack(b_n), jnp.stack(b_m),
               jnp.stack(c_kv), jnp.stack(c_kr))


def kernel(x_prompt, x_sample, c_prompt, c_sample, cache_a_k, cache_a_v, state_b_c, state_b_n, state_b_m, cache_c_kv, cache_c_kr, w_ada, b_ada, g_norm1, g_norm2, w_a_qkv, a_lambda, g_a_sub, w_a_o, w_b_in, b_b_gates, g_b_out, w_b_out, w_c_dq, g_c_q, w_c_uq, w_c_dkv, g_c_kv, w_c_ukv, w_c_o, w_ffn_in, w_ffn_out, g_final):
    p = dict(g_norm1=g_norm1, g_norm2=g_norm2, w_a_qkv=w_a_qkv, a_lambda=a_lambda, g_a_sub=g_a_sub,
             w_a_o=w_a_o, w_b_in=w_b_in, b_b_gates=b_b_gates, g_b_out=g_b_out, w_b_out=w_b_out,
             w_c_dq=w_c_dq, g_c_q=g_c_q, w_c_uq=w_c_uq, w_c_dkv=w_c_dkv, g_c_kv=g_c_kv, w_c_ukv=w_c_ukv,
             w_c_o=w_c_o, w_ffn_in=w_ffn_in, w_ffn_out=w_ffn_out, g_final=g_final)
    w = _prep_weights(p)
    nb_p = x_prompt.shape[0]
    mods = _ada(jnp.concatenate([c_prompt, c_sample], axis=0), w_ada, b_ada)
    y_prompt, sp = _trunk(x_prompt, mods[:, :nb_p], None, p, w)
    n_a, n_c = cache_a_k.shape[0], cache_c_kv.shape[0]
    dec_b, n_past = cache_a_k.shape[1], cache_a_k.shape[2]
    past = (cache_a_k.reshape(n_a, dec_b, n_past, D_MODEL), cache_a_v.reshape(n_a, dec_b, n_past, D_MODEL),
            state_b_c, state_b_n, state_b_m, cache_c_kv, cache_c_kr)
    y_sample, ss = _trunk(x_sample, mods[:, nb_p:], past, p, w)
    return (y_prompt, y_sample) + sp + ss
```

```python
import functools
import math

import jax
import jax.numpy as jnp
from jax import lax
from jax.experimental import pallas as pl
from jax.experimental.pallas import tpu as pltpu

F32 = jnp.float32
BF16 = jnp.bfloat16

D_MODEL = 1024
DEPTH = 4
CHUNK = 64
N_MIXERS = 3
ROPE_THETA = 500000.0
EPS = 1e-6
D_SUB = 64
H_A = D_MODEL // (2 * D_SUB)
ROT_A = D_SUB // 4
H_B = 4
DQK_B = D_MODEL // (2 * H_B)
DV_B = D_MODEL // H_B
H_C = 8
NOPE_C = 128
ROPE_C = 64
V_C = 128
Q_LORA = 384
KV_LORA = 256
D_FF = -(-8 * D_MODEL // (3 * 256)) * 256

LANES = 128
DQK_C_PAD = 2 * LANES
NEG = -1e30
VMEM_LIMIT = 56 * 1024 * 1024
TM = 512
TB = 256
HEAD_GROUP = 4
FF_CHUNK = 256
LOG2E = math.log2(math.e)
QSCALE_A = D_SUB ** -0.5 * LOG2E
QSCALE_C = (NOPE_C + ROPE_C) ** -0.5 * LOG2E


def _params(sem):
    return pltpu.CompilerParams(dimension_semantics=sem, vmem_limit_bytes=VMEM_LIMIT)


def _const(shape):
    nd = len(shape)
    return pl.BlockSpec(shape, lambda *_: (0,) * nd, pipeline_mode=pl.Buffered(1))


def _rms(x, g):
    return x * lax.rsqrt(jnp.mean(x * x, axis=-1, keepdims=True) + EPS) * g


def _dot(a, b):
    return jnp.dot(a, b, preferred_element_type=F32)


def _dot_nt(a, b):
    return lax.dot_general(a, b, (((1,), (1,)), ((), ())), preferred_element_type=F32)


def _rope_cols(c, cos, sin_lo, sin_hi, half):
    return c * cos + pltpu.roll(c, LANES - half, 1) * sin_lo + pltpu.roll(c, half, 1) * sin_hi


def _ada_kernel(c_ref, w_ref, b_ref, o_ref):
    c = c_ref[...]
    s = (c * jax.nn.sigmoid(c)).astype(BF16)
    o_ref[0] = _dot(s, w_ref[0].astype(BF16)) + b_ref[0]


def _ada(c_all, w_ada, b_ada):
    n = c_all.shape[0]
    tn = 1536
    return pl.pallas_call(
        _ada_kernel,
        grid=(DEPTH, 6 * D_MODEL // tn),
        in_specs=[pl.BlockSpec((n, D_MODEL), lambda i, j: (0, 0)),
                  pl.BlockSpec((1, D_MODEL, tn), lambda i, j: (i, 0, j)),
                  pl.BlockSpec((1, 1, tn), lambda i, j: (i, 0, j))],
        out_specs=pl.BlockSpec((1, n, tn), lambda i, j: (i, 0, j)),
        out_shape=jax.ShapeDtypeStruct((DEPTH, n, 6 * D_MODEL), F32),
        compiler_params=_params(("arbitrary", "arbitrary")),
        name="ada",
    )(c_all, w_ada, b_ada.reshape(DEPTH, 1, 6 * D_MODEL))


class _Path:
    def __init__(self, batch, seq):
        self.batch, self.seq = batch, seq
        self.n = batch * seq
        if seq >= TM:
            self.tm, self.tpb, self.mod_rows = TM, seq // TM, 1
        else:
            self.tm, self.tpb, self.mod_rows = self.n, 1, self.n
        self.steps = self.n // self.tm
        self.table_rows = self.tpb * self.tm

    def rows(self, width):
        return pl.BlockSpec((self.tm, width), lambda i: (i, 0))

    def mod(self):
        tpb = self.tpb
        return pl.BlockSpec((1, self.mod_rows, D_MODEL), lambda i: (i // tpb, 0, 0))

    def table(self):
        tpb = self.tpb
        return pl.BlockSpec((self.tm, LANES), lambda i: (i % tpb, 0))

    def expand_mod(self, m):
        if self.mod_rows == 1:
            return m[:, None, :]
        return jnp.repeat(m, self.seq, axis=0)[None]

    def expand_table(self, t):
        if self.table_rows == self.seq:
            return t
        return jnp.tile(t, (self.table_rows // self.seq, 1))


def _rope_tables(pos, rot, period):
    half = rot // 2
    inv = jnp.power(jnp.float32(ROPE_THETA), -jnp.arange(half, dtype=jnp.float32) * (2.0 / rot))
    ang = pos.astype(jnp.float32)[:, None] * inv[None, :]
    cos, sin = jnp.cos(ang), jnp.sin(ang)
    n = pos.shape[0]
    reps = LANES // period
    pad1 = jnp.ones((n, period - rot), F32)
    pad0 = jnp.zeros((n, period - rot), F32)
    zh = jnp.zeros((n, half), F32)
    cos_t = jnp.tile(jnp.concatenate([cos, cos, pad1], axis=1), (1, reps))
    sin_lo = jnp.tile(jnp.concatenate([-sin, zh, pad0], axis=1), (1, reps))
    sin_hi = jnp.tile(jnp.concatenate([zh, sin, pad0], axis=1), (1, reps))
    return cos_t, sin_lo, sin_hi


def _proj_a_kernel(*refs, transposed, aliased, tm):
    refs = refs[2:] if aliased else refs
    (x_ref, sh_ref, sc_ref, g_ref, w_ref, cos_ref, slo_ref, shi_ref,
     q_ref, k_ref, v_ref, krow_ref, vrow_ref) = refs
    h = (_rms(x_ref[...], g_ref[...]) * (1.0 + sc_ref[0]) + sh_ref[0]).astype(BF16)
    cos, slo, shi = cos_ref[...], slo_ref[...], shi_ref[...]
    q = _dot(h, w_ref[:, 0:D_MODEL])
    k = _dot(h, w_ref[:, D_MODEL:2 * D_MODEL])
    v = _dot(h, w_ref[:, 2 * D_MODEL:3 * D_MODEL])
    for j in range(H_A):
        cs = slice(j * LANES, (j + 1) * LANES)
        qj = _rope_cols(q[:, cs], cos, slo, shi, ROT_A // 2) * QSCALE_A
        kj = _rope_cols(k[:, cs], cos, slo, shi, ROT_A // 2)
        vj = v[:, cs]
        krow_ref[pl.ds(j, tm, stride=H_A), :] = kj
        vrow_ref[pl.ds(j, tm, stride=H_A), :] = vj
        k_ref[:, cs] = kj.astype(BF16)
        if transposed:
            qt, vt = qj.T.astype(BF16), vj.T.astype(BF16)
            for t in range(tm // TB):
                q_ref[0, t, cs, :] = qt[:, t * TB:(t + 1) * TB]
                v_ref[0, t, cs, :] = vt[:, t * TB:(t + 1) * TB]
        else:
            q_ref[:, cs] = qj.astype(BF16)
            v_ref[:, cs] = vj.astype(BF16)


def _proj_a(path, x, sh, sc, g, w_qkv, tables, layer_slot, n_slots, prev_rows):
    n, tm, tpb, steps = path.n, path.tm, path.tpb, path.steps
    transposed = path.seq >= TM
    aliased = prev_rows is not None
    bf = jax.ShapeDtypeStruct((n, D_MODEL), BF16)
    if transposed:
        bft = jax.ShapeDtypeStruct((path.batch, path.seq // TB, D_MODEL, TB), BF16)
        tspec = pl.BlockSpec((1, tm // TB, D_MODEL, TB), lambda i: (i // tpb, i % tpb, 0, 0))
        qv_shape, qv_spec = bft, tspec
    else:
        qv_shape, qv_spec = bf, path.rows(D_MODEL)
    rows_shape = jax.ShapeDtypeStruct((n_slots * n * H_A, LANES), F32)
    rows_spec = pl.BlockSpec((tm * H_A, LANES), lambda i: (layer_slot * steps + i, 0))
    in_specs = [path.rows(D_MODEL), path.mod(), path.mod(), _const((1, D_MODEL)),
                _const((D_MODEL, 3 * D_MODEL)), path.table(), path.table(), path.table()]
    args = [x, sh, sc, g, w_qkv, *tables]
    aliases = {}
    if aliased:
        in_specs = [pl.BlockSpec(memory_space=pl.ANY)] * 2 + in_specs
        args = list(prev_rows) + args
        aliases = {0: 3, 1: 4}
    return pl.pallas_call(
        functools.partial(_proj_a_kernel, transposed=transposed, aliased=aliased, tm=tm),
        grid=(steps,),
        in_specs=in_specs,
        out_specs=[qv_spec, path.rows(D_MODEL), qv_spec, rows_spec, rows_spec],
        out_shape=[qv_shape, bf, qv_shape, rows_shape, rows_shape],
        input_output_aliases=aliases,
        compiler_params=_params(("arbitrary",)),
        name="proj_a",
    )(*args)


def _lambda_full(lam_ref, lambda_init):
    lv = lam_ref[0]
    d1 = jnp.sum(lv[0:1] * lv[1:2], axis=-1, keepdims=True)
    d2 = jnp.sum(lv[2:3] * lv[3:4], axis=-1, keepdims=True)
    return jnp.exp(d1) - jnp.exp(d2) + lambda_init


def _split_subheads(q):
    lane = lax.broadcasted_iota(jnp.int32, q.shape, 1)
    zero = jnp.zeros_like(q)
    return [jnp.where(lane < D_SUB, q, zero), jnp.where(lane >= D_SUB, q, zero)]


def _diff_combine(o1, o2, lam, g_sub, lambda_init):
    o = o1 - lam * o2
    return _rms(o, g_sub) * (1.0 - lambda_init)


def _flash_kernel(*refs, diff, lambda_init, n_tiles, heads, dqk):
    if diff:
        qt_ref, k_ref, vt_ref, lam_ref, gsub_ref, o_ref, m_scr, l_scr, acc_scr = refs
    else:
        qt_ref, k_ref, vt_ref, o_ref, m_scr, l_scr, acc_scr = refs
    n_sub = 2 if diff else 1
    key = lax.broadcasted_iota(jnp.int32, (TB, TB), 0)
    qry = lax.broadcasted_iota(jnp.int32, (TB, TB), 1)
    visible = (key // CHUNK) <= (qry // CHUNK)

    def q_tile(i, carry):
        qs = []
        for h in range(heads):
            qt = qt_ref[0, i, h * dqk:(h + 1) * dqk, :]
            if diff:
                dim = lax.broadcasted_iota(jnp.int32, qt.shape, 0)
                zero = jnp.zeros_like(qt)
                qs += [jnp.where(dim < D_SUB, qt, zero), jnp.where(dim >= D_SUB, qt, zero)]
            else:
                qs.append(qt)
        m_scr[...] = jnp.full(m_scr.shape, NEG, F32)
        l_scr[...] = jnp.zeros(l_scr.shape, F32)
        acc_scr[...] = jnp.zeros(acc_scr.shape, F32)

        def block(j, masked):
            rows = pl.ds(pl.multiple_of(j * TB, TB), TB)
            scores = []
            for h in range(heads):
                kb = k_ref[0, rows, h * dqk:(h + 1) * dqk]
                for s in range(n_sub):
                    scores.append(_dot(kb, qs[h * n_sub + s]))
            for h in range(heads):
                vb = vt_ref[0, j, h * LANES:(h + 1) * LANES, :]
                for s in range(n_sub):
                    c = h * n_sub + s
                    st = jnp.where(visible, scores[c], NEG) if masked else scores[c]
                    m_prev = m_scr[c]
                    m_new = jnp.maximum(m_prev, jnp.max(st, axis=0, keepdims=True))
                    alpha = jnp.exp2(m_prev - m_new)
                    p = jnp.exp2(st - m_new)
                    l_scr[c] = alpha * l_scr[c] + jnp.sum(p, axis=0, keepdims=True)
                    acc_scr[c] = alpha * acc_scr[c] + _dot(vb, p.astype(BF16))
                    m_scr[c] = m_new

        def body(j, c):
            block(j, False)
            return c

        lax.fori_loop(0, i, body, 0)
        block(i, True)
        out_rows = pl.ds(pl.multiple_of(i * TB, TB), TB)
        for h in range(heads):
            if diff:
                c = 2 * h
                ot = acc_scr[c] / l_scr[c] - _lambda_full(lam_ref, lambda_init) * (acc_scr[c + 1] / l_scr[c + 1])
                ot = (ot * lax.rsqrt(jnp.mean(ot * ot, axis=0, keepdims=True) + EPS) * gsub_ref[0]
                      * (1.0 - lambda_init))
            else:
                ot = acc_scr[h] / l_scr[h]
            o_ref[0, out_rows, h * LANES:(h + 1) * LANES] = ot.T.astype(BF16)
        return carry

    lax.fori_loop(0, n_tiles, q_tile, 0)


def _flash(qt, k, vt, batch, seq, heads, dqk, diff_args=None, lambda_init=0.0):
    diff = diff_args is not None
    hg = HEAD_GROUP
    n_tiles = seq // TB
    chains = hg * (2 if diff else 1)
    in_specs = [pl.BlockSpec((1, n_tiles, hg * dqk, TB), lambda b, g: (b, 0, g, 0)),
                pl.BlockSpec((1, seq, hg * dqk), lambda b, g: (b, 0, g)),
                pl.BlockSpec((1, n_tiles, hg * LANES, TB), lambda b, g: (b, 0, g, 0))]
    args = [qt, k, vt]
    if diff:
        lam, g_sub_t, jl = diff_args
        in_specs += [pl.BlockSpec((1, 4, D_SUB), lambda b, g: (jl, 0, 0)),
                     pl.BlockSpec((1, LANES, TB), lambda b, g: (jl, 0, 0))]
        args += [lam, g_sub_t]
    return pl.pallas_call(
        functools.partial(_flash_kernel, diff=diff, lambda_init=lambda_init, n_tiles=n_tiles, heads=hg, dqk=dqk),
        grid=(batch, heads // hg),
        in_specs=in_specs,
        out_specs=pl.BlockSpec((1, seq, hg * LANES), lambda b, g: (b, 0, g)),
        out_shape=jax.ShapeDtypeStruct((batch, seq, heads * LANES), BF16),
        scratch_shapes=[pltpu.VMEM((chains, 1, TB), F32), pltpu.VMEM((chains, 1, TB), F32),
                        pltpu.VMEM((chains, LANES, TB), F32)],
        compiler_params=_params(("arbitrary", "arbitrary")),
        name="flash_diff" if diff else "flash_mla",
    )(*args)


def _softmax_two(s_past, s_new):
    m = jnp.maximum(jnp.max(s_past, axis=-1, keepdims=True), jnp.max(s_new, axis=-1, keepdims=True))
    p_past = jnp.exp2(s_past - m)
    p_new = jnp.exp2(s_new - m)
    l = jnp.sum(p_past, axis=-1, keepdims=True) + jnp.sum(p_new, axis=-1, keepdims=True)
    return p_past, p_new, l


def _decode_a_kernel(q_ref, kn_ref, vn_ref, kp_ref, vp_ref, lam_ref, gsub_ref, o_ref, *, lambda_init):
    qs = _split_subheads(q_ref[0])
    kp = kp_ref[0, 0].astype(BF16)
    vp = vp_ref[0, 0].astype(BF16)
    kn, vn = kn_ref[0], vn_ref[0]
    outs = []
    for qsub in qs:
        p_past, p_new, l = _softmax_two(_dot_nt(qsub, kp), _dot_nt(qsub, kn))
        outs.append((_dot(p_past.astype(BF16), vp) + _dot(p_new.astype(BF16), vn)) / l)
    o = _diff_combine(outs[0], outs[1], _lambda_full(lam_ref, lambda_init), gsub_ref[0], lambda_init)
    o_ref[0] = o.astype(BF16)


def _decode_a(q, k, v, cache_k, cache_v, lam, g_sub, jl, lambda_init, batch, seq):
    past = cache_k.shape[2]
    new = lambda b, h: (b, 0, h)
    old = lambda b, h: (jl, b, 0, h)
    return pl.pallas_call(
        functools.partial(_decode_a_kernel, lambda_init=lambda_init),
        grid=(batch, H_A),
        in_specs=[pl.BlockSpec((1, seq, LANES), new), pl.BlockSpec((1, seq, LANES), new),
                  pl.BlockSpec((1, seq, LANES), new),
                  pl.BlockSpec((1, 1, past, LANES), old), pl.BlockSpec((1, 1, past, LANES), old),
                  pl.BlockSpec((1, 4, D_SUB), lambda b, h: (jl, 0, 0)),
                  pl.BlockSpec((1, 1, LANES), lambda b, h: (jl, 0, 0))],
        out_specs=pl.BlockSpec((1, seq, LANES), new),
        out_shape=jax.ShapeDtypeStruct((batch, seq, D_MODEL), BF16),
        compiler_params=_params(("arbitrary", "arbitrary")),
        name="decode_a",
    )(q, k, v, cache_k, cache_v, lam, g_sub)


def _proj_c_kernel(x_ref, sh_ref, sc_ref, g_ref, w1_ref, gq_ref, wuq_ref, gkv_ref, wukv_ref,
                   cos_ref, slo_ref, shi_ref, q_ref, k_ref, v_ref, kvrow_ref, krrow_ref, *, transposed, tm):
    h = (_rms(x_ref[...], g_ref[...]) * (1.0 + sc_ref[0]) + sh_ref[0]).astype(BF16)
    cos, slo, shi = cos_ref[...], slo_ref[...], shi_ref[...]
    d = _dot(h, w1_ref[...])
    qlat = _rms(d[:, 0:Q_LORA], gq_ref[...]).astype(BF16)
    kv_row = _rms(d[:, Q_LORA:Q_LORA + KV_LORA], gkv_ref[...])
    kr = _rope_cols(d[:, Q_LORA + KV_LORA:], cos, slo, shi, ROPE_C // 2)
    kvrow_ref[...] = kv_row
    krrow_ref[...] = kr[:, 0:ROPE_C]
    kr_bf = kr.astype(BF16)
    q = _dot(qlat, wuq_ref[...]) * QSCALE_C
    kv = _dot(kv_row.astype(BF16), wukv_ref[...])

    def put(ref, cs, val):
        if transposed:
            vt = val.T.astype(BF16)
            for t in range(tm // TB):
                ref[0, t, cs, :] = vt[:, t * TB:(t + 1) * TB]
        else:
            ref[:, cs] = val.astype(BF16)

    for hh in range(H_C):
        lo = slice(hh * DQK_C_PAD, hh * DQK_C_PAD + LANES)
        hi = slice(hh * DQK_C_PAD + LANES, (hh + 1) * DQK_C_PAD)
        put(q_ref, lo, q[:, lo])
        put(q_ref, hi, _rope_cols(q[:, hi], cos, slo, shi, ROPE_C // 2))
        k_ref[:, lo] = kv[:, lo].astype(BF16)
        k_ref[:, hi] = kr_bf
        put(v_ref, slice(hh * LANES, (hh + 1) * LANES), kv[:, hi])


def _proj_c(path, x, sh, sc, g, w1, g_q, w_uq, g_kv, w_ukv, tables):
    n, tm, tpb = path.n, path.tm, path.tpb
    w1n = w1.shape[1]
    transposed = path.seq >= TM
    if transposed:
        tiled = lambda width: (jax.ShapeDtypeStruct((path.batch, path.seq // TB, width, TB), BF16),
                               pl.BlockSpec((1, tm // TB, width, TB), lambda i: (i // tpb, i % tpb, 0, 0)))
    else:
        tiled = lambda width: (jax.ShapeDtypeStruct((n, width), BF16), path.rows(width))
    q_shape, q_spec = tiled(H_C * DQK_C_PAD)
    v_shape, v_spec = tiled(H_C * V_C)
    return pl.pallas_call(
        functools.partial(_proj_c_kernel, transposed=transposed, tm=tm),
        grid=(path.steps,),
        in_specs=[path.rows(D_MODEL), path.mod(), path.mod(), _const((1, D_MODEL)),
                  _const((D_MODEL, w1n)), _const((1, Q_LORA)), _const((Q_LORA, H_C * DQK_C_PAD)),
                  _const((1, KV_LORA)), _const((KV_LORA, H_C * (NOPE_C + V_C))),
                  path.table(), path.table(), path.table()],
        out_specs=[q_spec, path.rows(H_C * DQK_C_PAD), v_spec, path.rows(KV_LORA), path.rows(ROPE_C)],
        out_shape=[q_shape, jax.ShapeDtypeStruct((n, H_C * DQK_C_PAD), BF16), v_shape,
                   jax.ShapeDtypeStruct((n, KV_LORA), F32), jax.ShapeDtypeStruct((n, ROPE_C), F32)],
        compiler_params=_params(("arbitrary",)),
        name="proj_c",
    )(x, sh, sc, g, w1, g_q, w_uq, g_kv, w_ukv, *tables)


def _decode_c_kernel(q_ref, kn_ref, vn_ref, kvp_ref, krp_ref, wukv_ref, o_ref):
    q = q_ref[0]
    kvh = _dot(kvp_ref[0, 0].astype(BF16), wukv_ref[...])
    k_past = kvh[:, 0:NOPE_C].astype(BF16)
    v_past = kvh[:, NOPE_C:].astype(BF16)
    kr_past = krp_ref[0, 0].astype(BF16)
    s_past = _dot_nt(q[:, 0:NOPE_C], k_past) + _dot_nt(q[:, NOPE_C:NOPE_C + ROPE_C], kr_past)
    p_past, p_new, l = _softmax_two(s_past, _dot_nt(q, kn_ref[0]))
    o = (_dot(p_past.astype(BF16), v_past) + _dot(p_new.astype(BF16), vn_ref[0])) / l
    o_ref[0] = o.astype(BF16)


def _decode_c(q, k, v, cache_kv, cache_kr, w_ukv, jl, batch, seq):
    past = cache_kv.shape[2]
    return pl.pallas_call(
        _decode_c_kernel,
        grid=(batch, H_C),
        in_specs=[pl.BlockSpec((1, seq, DQK_C_PAD), lambda b, h: (b, 0, h)),
                  pl.BlockSpec((1, seq, DQK_C_PAD), lambda b, h: (b, 0, h)),
                  pl.BlockSpec((1, seq, LANES), lambda b, h: (b, 0, h)),
                  pl.BlockSpec((1, 1, past, KV_LORA), lambda b, h: (jl, b, 0, 0)),
                  pl.BlockSpec((1, 1, past, ROPE_C), lambda b, h: (jl, b, 0, 0)),
                  pl.BlockSpec((KV_LORA, NOPE_C + V_C), lambda b, h: (0, h))],
        out_specs=pl.BlockSpec((1, seq, LANES), lambda b, h: (b, 0, h)),
        out_shape=jax.ShapeDtypeStruct((batch, seq, H_C * V_C), BF16),
        compiler_params=_params(("arbitrary", "arbitrary")),
        name="decode_c",
    )(q, k, v, cache_kv, cache_kr, w_ukv)


def _proj_b_kernel(x_ref, sh_ref, sc_ref, g_ref, w_ref, bg_ref,
                   q_ref, k_ref, kt_ref, v_ref, opre_ref, gates_ref, gt_ref):
    nqk, nv = H_B * DQK_B, H_B * DV_B
    h = (_rms(x_ref[...], g_ref[...]) * (1.0 + sc_ref[0]) + sh_ref[0]).astype(BF16)
    q_ref[...] = _dot(h, w_ref[:, 0:nqk]).astype(BF16)
    k = _dot(h, w_ref[:, nqk:2 * nqk]) * (DQK_B ** -0.5)
    k_ref[...] = k
    kt_ref[0] = k.T
    v_ref[...] = _dot(h, w_ref[:, 2 * nqk:2 * nqk + nv]).astype(BF16)
    opre_ref[...] = _dot(h, w_ref[:, 2 * nqk + nv:2 * nqk + 2 * nv])
    gates = _dot(h, w_ref[:, 2 * nqk + 2 * nv:]) + bg_ref[...]
    gates_ref[...] = gates
    gt_ref[0] = gates.T[0:2 * H_B, :]


def _proj_b(path, x, sh, sc, g, w_in, b_gates):
    n, nqk, nv = path.n, H_B * DQK_B, H_B * DV_B
    groups = path.n // path.table_rows
    tpb = path.tpb
    cols = lambda width: pl.BlockSpec((1, width, path.tm), lambda i: (i // tpb, 0, i % tpb))
    return pl.pallas_call(
        _proj_b_kernel,
        grid=(path.steps,),
        in_specs=[path.rows(D_MODEL), path.mod(), path.mod(), _const((1, D_MODEL)),
                  _const((D_MODEL, w_in.shape[1])), _const((1, LANES))],
        out_specs=[path.rows(nqk), path.rows(nqk), cols(nqk), path.rows(nv), path.rows(nv),
                   path.rows(LANES), cols(2 * H_B)],
        out_shape=[jax.ShapeDtypeStruct((n, nqk), BF16), jax.ShapeDtypeStruct((n, nqk), F32),
                   jax.ShapeDtypeStruct((groups, nqk, path.table_rows), F32),
                   jax.ShapeDtypeStruct((n, nv), BF16), jax.ShapeDtypeStruct((n, nv), F32),
                   jax.ShapeDtypeStruct((n, LANES), F32),
                   jax.ShapeDtypeStruct((groups, 2 * H_B, path.table_rows), F32)],
        compiler_params=_params(("arbitrary",)),
        name="proj_b",
    )(x, sh, sc, g, w_in, b_gates)


def _mlstm_kernel(q_ref, k_ref, kt_ref, v_ref, g_ref, gt_ref, c0_ref, n0_ref, m0_ref,
                  hs_ref, ct_ref, nt_ref, mt_ref, c_scr, n_scr, m_scr, *, chunk, chunks_per_step):
    step = pl.program_id(1)

    @pl.when(step == 0)
    def _():
        c_scr[...] = c0_ref[0]
        n_scr[...] = n0_ref[0]
        m_scr[...] = m0_ref[0]

    L = chunk
    row = lax.broadcasted_iota(jnp.int32, (L, L), 0)
    col = lax.broadcasted_iota(jnp.int32, (L, L), 1)
    tril = col <= row
    triu = row <= col
    for c in range(chunks_per_step):
        sl = slice(c * L, (c + 1) * L)
        g_col = g_ref[0, sl, :]
        g_row = gt_ref[0, :, sl]
        lf_col = jax.nn.log_sigmoid(g_col)
        lf_row = jax.nn.log_sigmoid(g_row)
        for h in range(H_B):
            ig_col, f_col = g_col[:, h:h + 1], lf_col[:, H_B + h:H_B + h + 1]
            ig_row, f_row = g_row[h:h + 1, :], lf_row[H_B + h:H_B + h + 1, :]
            b_col = jnp.sum(jnp.where(tril, f_row, 0.0), axis=1, keepdims=True)
            b_row = jnp.sum(jnp.where(triu, f_col, 0.0), axis=0, keepdims=True)
            g_tot = jnp.sum(f_row, axis=1, keepdims=True)
            m_prev = m_scr[h]
            dm = jnp.where(tril, b_col - b_row + ig_row, NEG)
            inter = b_col + m_prev
            m_t = jnp.maximum(inter, jnp.max(dm, axis=1, keepdims=True))
            w_ts = jnp.exp(dm - m_t)
            a = jnp.exp(inter - m_t)
            qh = q_ref[0, sl, h * DQK_B:(h + 1) * DQK_B]
            kh = k_ref[0, sl, h * DQK_B:(h + 1) * DQK_B]
            vh = v_ref[0, sl, h * DV_B:(h + 1) * DV_B]
            c_prev = c_scr[h]
            n_prev = n_scr[h]
            qk = _dot_nt(qh, kh.astype(BF16)) * w_ts
            num = a * _dot(qh, c_prev.astype(BF16)) + _dot(qk.astype(BF16), vh)
            qn = jnp.sum(qh.astype(F32) * n_prev, axis=1, keepdims=True)
            den = a * qn + jnp.sum(qk, axis=1, keepdims=True)
            hs_ref[0, sl, h * DV_B:(h + 1) * DV_B] = num / jnp.maximum(jnp.abs(den), jnp.exp(-m_t))
            r_row = g_tot - b_row + ig_row
            r_col = g_tot - b_col + ig_col
            m_new = jnp.maximum(g_tot + m_prev, jnp.max(r_row, axis=1, keepdims=True))
            decay = jnp.exp(g_tot + m_prev - m_new)
            kw_t = kt_ref[0, h * DQK_B:(h + 1) * DQK_B, sl] * jnp.exp(r_row - m_new)
            c_scr[h] = decay * c_prev + _dot(kw_t.astype(BF16), vh)
            n_scr[h] = decay * n_prev + jnp.sum(kh * jnp.exp(r_col - m_new), axis=0, keepdims=True)
            m_scr[h] = m_new

    @pl.when(step == pl.num_programs(1) - 1)
    def _():
        ct_ref[0] = c_scr[...]
        nt_ref[0] = n_scr[...]
        mt_ref[0] = m_scr[...]


def _mlstm(q, k, kt, v, gates, gt, c0, n0, m0, batch, seq):
    chunk = CHUNK if seq % CHUNK == 0 else seq
    cps = max(1, min(LANES // chunk, seq // chunk))
    span = chunk * cps
    nqk, nv = H_B * DQK_B, H_B * DV_B
    rows = lambda width: pl.BlockSpec((1, span, width), lambda b, s: (b, s, 0))
    cols = lambda height: pl.BlockSpec((1, height, span), lambda b, s: (b, 0, s))
    state = lambda *shape: pl.BlockSpec((1,) + shape, lambda b, s: (b,) + (0,) * len(shape))
    return pl.pallas_call(
        functools.partial(_mlstm_kernel, chunk=chunk, chunks_per_step=cps),
        grid=(batch, seq // span),
        in_specs=[rows(nqk), rows(nqk), cols(nqk), rows(nv), rows(LANES), cols(2 * H_B),
                  state(H_B, DQK_B, DV_B), state(H_B, 1, DQK_B), state(H_B, 1, 1)],
        out_specs=[rows(nv), state(H_B, DQK_B, DV_B), state(H_B, 1, DQK_B), state(H_B, 1, 1)],
        out_shape=[jax.ShapeDtypeStruct((batch, seq, nv), F32),
                   jax.ShapeDtypeStruct((batch, H_B, DQK_B, DV_B), F32),
                   jax.ShapeDtypeStruct((batch, H_B, 1, DQK_B), F32),
                   jax.ShapeDtypeStruct((batch, H_B, 1, 1), F32)],
        scratch_shapes=[pltpu.VMEM((H_B, DQK_B, DV_B), F32), pltpu.VMEM((H_B, 1, DQK_B), F32),
                        pltpu.VMEM((H_B, 1, 1), F32)],
        compiler_params=_params(("arbitrary", "arbitrary")),
        name="mlstm",
    )(q, k, kt, v, gates, gt, c0, n0, m0)


def _out_kernel(o_ref, x_ref, gt_ref, w_ref, y_ref):
    y_ref[...] = x_ref[...] + gt_ref[0] * _dot(o_ref[...], w_ref[...])


def _out_b_kernel(hs_ref, opre_ref, gout_ref, x_ref, gt_ref, w_ref, y_ref):
    hs = hs_ref[...]
    gout = gout_ref[...]
    cols = []
    for h in range(H_B):
        cs = slice(h * DV_B, (h + 1) * DV_B)
        cols.append(_rms(hs[:, cs], gout[:, cs]))
    z = (jax.nn.sigmoid(opre_ref[...]) * jnp.concatenate(cols, axis=-1)).astype(BF16)
    y_ref[...] = x_ref[...] + gt_ref[0] * _dot(z, w_ref[...])


def _out_proj(path, o, x, gt, w):
    return pl.pallas_call(
        _out_kernel,
        grid=(path.steps,),
        in_specs=[path.rows(D_MODEL), path.rows(D_MODEL), path.mod(), _const((D_MODEL, D_MODEL))],
        out_specs=path.rows(D_MODEL),
        out_shape=jax.ShapeDtypeStruct((path.n, D_MODEL), F32),
        compiler_params=_params(("arbitrary",)),
        name="out_proj",
    )(o, x, gt, w)


def _out_proj_b(path, hs, opre, g_out, x, gt, w):
    return pl.pallas_call(
        _out_b_kernel,
        grid=(path.steps,),
        in_specs=[path.rows(D_MODEL), path.rows(D_MODEL), _const((1, D_MODEL)), path.rows(D_MODEL),
                  path.mod(), _const((D_MODEL, D_MODEL))],
        out_specs=path.rows(D_MODEL),
        out_shape=jax.ShapeDtypeStruct((path.n, D_MODEL), F32),
        compiler_params=_params(("arbitrary",)),
        name="out_proj_b",
    )(hs, opre, g_out, x, gt, w)


def _ffn_kernel(x_ref, sh_ref, sc_ref, gt_ref, g_ref, win_ref, wout_ref, gfin_ref, y_ref, acc_scr, *, final):
    x = x_ref[...]
    h = (_rms(x, g_ref[...]) * (1.0 + sc_ref[0]) + sh_ref[0]).astype(BF16)
    for c in range(D_FF // FF_CHUNK):
        a = _dot(h, win_ref[:, c * FF_CHUNK:(c + 1) * FF_CHUNK])
        b = _dot(h, win_ref[:, D_FF + c * FF_CHUNK:D_FF + (c + 1) * FF_CHUNK])
        act = (a * jax.nn.sigmoid(a) * b).astype(BF16)
        part = _dot(act, wout_ref[c * FF_CHUNK:(c + 1) * FF_CHUNK, :])
        if c == 0:
            acc_scr[...] = part
        else:
            acc_scr[...] += part
    y = x + gt_ref[0] * acc_scr[...]
    y_ref[...] = _rms(y, gfin_ref[...]) if final else y


def _ffn(path, x, sh, sc, gt, g, w_in, w_out, g_final, final):
    return pl.pallas_call(
        functools.partial(_ffn_kernel, final=final),
        grid=(path.steps,),
        in_specs=[path.rows(D_MODEL), path.mod(), path.mod(), path.mod(), _const((1, D_MODEL)),
                  _const((D_MODEL, 2 * D_FF)), _const((D_FF, D_MODEL)), _const((1, D_MODEL))],
        out_specs=path.rows(D_MODEL),
        out_shape=jax.ShapeDtypeStruct((path.n, D_MODEL), F32),
        scratch_shapes=[pltpu.VMEM((path.tm, D_MODEL), F32)],
        compiler_params=_params(("arbitrary",)),
        name="ffn",
    )(x, sh, sc, gt, g, w_in, w_out, g_final)


def _prep_weights(p):
    w = {}
    w['a_qkv'] = p['w_a_qkv'].astype(BF16)
    w['a_o'] = p['w_a_o'].astype(BF16)
    nqk, nv = H_B * DQK_B, H_B * DV_B
    w_b = p['w_b_in']
    gate_pad = jnp.zeros(w_b.shape[:2] + (LANES - 2 * H_B,), w_b.dtype)
    w['b_in'] = jnp.concatenate([w_b, gate_pad], axis=-1).astype(BF16)
    bg = p['b_b_gates']
    w['b_gates'] = jnp.concatenate([bg, jnp.zeros((bg.shape[0], LANES - 2 * H_B), bg.dtype)], axis=-1)[:, None, :]
    w['b_out'] = p['w_b_out'].astype(BF16)
    n_c = p['w_c_dq'].shape[0]
    w_dkv = p['w_c_dkv']
    w['c_1'] = jnp.concatenate([p['w_c_dq'], w_dkv, jnp.zeros((n_c, D_MODEL, LANES - ROPE_C), w_dkv.dtype)],
                               axis=-1).astype(BF16)
    w_uq = p['w_c_uq'].reshape(n_c, Q_LORA, H_C, NOPE_C + ROPE_C)
    w_uq = jnp.concatenate([w_uq, jnp.zeros((n_c, Q_LORA, H_C, DQK_C_PAD - NOPE_C - ROPE_C), w_uq.dtype)], axis=-1)
    w['c_uq'] = w_uq.reshape(n_c, Q_LORA, H_C * DQK_C_PAD).astype(BF16)
    w['c_ukv'] = p['w_c_ukv'].astype(BF16)
    w['c_o'] = p['w_c_o'].astype(BF16)
    w['ffn_in'] = p['w_ffn_in'].astype(BF16)
    w['ffn_out'] = p['w_ffn_out'].astype(BF16)
    return w


def _trunk(x, mods, past, p, w):
    batch, seq, _ = x.shape
    path = _Path(batch, seq)
    n_past = 0 if past is None else past[0].shape[2]
    pos = n_past + jnp.arange(seq, dtype=jnp.int32)
    tab_a = tuple(path.expand_table(t) for t in _rope_tables(pos, ROT_A, D_SUB))
    tab_c = tuple(path.expand_table(t) for t in _rope_tables(pos, ROPE_C, LANES))
    x = x.reshape(path.n, D_MODEL)
    g_final = p['g_final'][None, :]
    n_a = (DEPTH + N_MIXERS - 1) // N_MIXERS
    a_rows = None
    b_c, b_n, b_m, c_kv, c_kr = [], [], [], [], []
    for i in range(DEPTH):
        kind, j = i % N_MIXERS, i // N_MIXERS
        sh1, sc1, gt1, sh2, sc2, gt2 = (path.expand_mod(m) for m in jnp.split(mods[i], 6, axis=-1))
        g1 = p['g_norm1'][i][None, :]
        if kind == 0:
            lambda_init = 0.8 - 0.6 * math.exp(-0.3 * i)
            q, k, v, k_row, v_row = _proj_a(path, x, sh1, sc1, g1, w['a_qkv'][j], tab_a, j, n_a, a_rows)
            a_rows = (k_row, v_row)
            shape3 = (batch, seq, D_MODEL)
            if past is None:
                g_sub_t = jnp.broadcast_to(p['g_a_sub'][:, :, None], p['g_a_sub'].shape + (TB,))
                o = _flash(q, k.reshape(shape3), v, batch, seq, H_A, LANES,
                           diff_args=(p['a_lambda'], g_sub_t, j), lambda_init=lambda_init)
            else:
                o = _decode_a(q.reshape(shape3), k.reshape(shape3), v.reshape(shape3), past[0], past[1],
                              p['a_lambda'], p['g_a_sub'][:, None, :], j, lambda_init, batch, seq)
            x = _out_proj(path, o.reshape(path.n, D_MODEL), x, gt1, w['a_o'][j])
        elif kind == 1:
            q, k, kt, v, opre, gates, gt = _proj_b(path, x, sh1, sc1, g1, w['b_in'][j], w['b_gates'][j])
            nqk, nv = H_B * DQK_B, H_B * DV_B
            if path.table_rows != seq:
                kt = kt.reshape(nqk, batch, seq).transpose(1, 0, 2)
                gt = gt.reshape(2 * H_B, batch, seq).transpose(1, 0, 2)
            if past is None:
                c0 = jnp.zeros((batch, H_B, DQK_B, DV_B), F32)
                n0 = jnp.zeros((batch, H_B, 1, DQK_B), F32)
                m0 = jnp.zeros((batch, H_B, 1, 1), F32)
            else:
                c0 = past[2][j]
                n0 = past[3][j][:, :, None, :]
                m0 = past[4][j][:, :, None, None]
            hs, c_t, n_t, m_t = _mlstm(q.reshape(batch, seq, nqk), k.reshape(batch, seq, nqk), kt,
                                       v.reshape(batch, seq, nv), gates.reshape(batch, seq, LANES), gt,
                                       c0, n0, m0, batch, seq)
            b_c.append(c_t)
            b_n.append(n_t[:, :, 0, :])
            b_m.append(m_t[:, :, 0, 0])
            x = _out_proj_b(path, hs.reshape(path.n, nv), opre, p['g_b_out'][j][None, :], x, gt1, w['b_out'][j])
        else:
            q, k, v, kv_row, kr_row = _proj_c(path, x, sh1, sc1, g1, w['c_1'][j], p['g_c_q'][j][None, :],
                                              w['c_uq'][j], p['g_c_kv'][j][None, :], w['c_ukv'][j], tab_c)
            qk_shape = (batch, seq, H_C * DQK_C_PAD)
            v_shape = (batch, seq, H_C * V_C)
            if past is None:
                o = _flash(q, k.reshape(qk_shape), v, batch, seq, H_C, DQK_C_PAD)
            else:
                o = _decode_c(q.reshape(qk_shape), k.reshape(qk_shape), v.reshape(v_shape), past[5], past[6],
                              w['c_ukv'][j], j, batch, seq)
            c_kv.append(kv_row.reshape(batch, seq, KV_LORA))
            c_kr.append(kr_row.reshape(batch, seq, ROPE_C))
            x = _out_proj(path, o.reshape(path.n, D_MODEL), x, gt1, w['c_o'][j])
        x = _ffn(path, x, sh2, sc2, gt2, p['g_norm2'][i][None, :], w['ffn_in'][i], w['ffn_out'][i],
                 g_final, final=(i == DEPTH - 1))
    y = x.reshape(batch, seq, D_MODEL)
    a_k, a_v = (r.reshape(n_a, batch, seq, H_A, 2 * D_SUB) for r in a_rows)
    return y, (a_k, a_v, jnp.stack(b_c), jnp.stack(b_n), jnp.stack(b_m), jnp.stack(c_kv), jnp.stack(c_kr))


def kernel(x_prompt, x_sample, c_prompt, c_sample, cache_a_k, cache_a_v, state_b_c, state_b_n, state_b_m, cache_c_kv, cache_c_kr, w_ada, b_ada, g_norm1, g_norm2, w_a_qkv, a_lambda, g_a_sub, w_a_o, w_b_in, b_b_gates, g_b_out, w_b_out, w_c_dq, g_c_q, w_c_uq, w_c_dkv, g_c_kv, w_c_ukv, w_c_o, w_ffn_in, w_ffn_out, g_final):
    p = dict(g_norm1=g_norm1, g_norm2=g_norm2, w_a_qkv=w_a_qkv, a_lambda=a_lambda, g_a_sub=g_a_sub,
             w_a_o=w_a_o, w_b_in=w_b_in, b_b_gates=b_b_gates, g_b_out=g_b_out, w_b_out=w_b_out,
             w_c_dq=w_c_dq, g_c_q=g_c_q, w_c_uq=w_c_uq, w_c_dkv=w_c_dkv, g_c_kv=g_c_kv, w_c_ukv=w_c_ukv,
             w_c_o=w_c_o, w_ffn_in=w_ffn_in, w_ffn_out=w_ffn_out, g_final=g_final)
    w = _prep_weights(p)
    nb_p = x_prompt.shape[0]
    mods = _ada(jnp.concatenate([c_prompt, c_sample], axis=0), w_ada, b_ada)
    y_prompt, sp = _trunk(x_prompt, mods[:, :nb_p], None, p, w)
    n_a, n_c = cache_a_k.shape[0], cache_c_kv.shape[0]
    dec_b, n_past = cache_a_k.shape[1], cache_a_k.shape[2]
    past = (cache_a_k.reshape(n_a, dec_b, n_past, D_MODEL), cache_a_v.reshape(n_a, dec_b, n_past, D_MODEL),
            state_b_c, state_b_n, state_b_m, cache_c_kv, cache_c_kr)
    y_sample, ss = _trunk(x_sample, mods[:, nb_p:], past, p, w)
    return (y_prompt, y_sample) + sp + ss
```

```python
import functools
import math

import jax
import jax.numpy as jnp
from jax import lax
from jax.experimental import pallas as pl
from jax.experimental.pallas import tpu as pltpu

F32 = jnp.float32
BF16 = jnp.bfloat16

D_MODEL = 1024
DEPTH = 4
CHUNK = 64
N_MIXERS = 3
ROPE_THETA = 500000.0
EPS = 1e-6
D_SUB = 64
H_A = D_MODEL // (2 * D_SUB)
ROT_A = D_SUB // 4
H_B = 4
DQK_B = D_MODEL // (2 * H_B)
DV_B = D_MODEL // H_B
H_C = 8
NOPE_C = 128
ROPE_C = 64
V_C = 128
Q_LORA = 384
KV_LORA = 256
D_FF = -(-8 * D_MODEL // (3 * 256)) * 256

LANES = 128
DQK_C_PAD = 2 * LANES
NEG = -1e30
VMEM_LIMIT = 56 * 1024 * 1024
TM = 512
TB = 256
HEAD_GROUP = 4
FF_CHUNK = 256
LOG2E = math.log2(math.e)
QSCALE_A = D_SUB ** -0.5 * LOG2E
QSCALE_C = (NOPE_C + ROPE_C) ** -0.5 * LOG2E


def _params(sem):
    return pltpu.CompilerParams(dimension_semantics=sem, vmem_limit_bytes=VMEM_LIMIT)


def _const(shape):
    nd = len(shape)
    return pl.BlockSpec(shape, lambda *_: (0,) * nd, pipeline_mode=pl.Buffered(1))


def _rms(x, g):
    return x * lax.rsqrt(jnp.mean(x * x, axis=-1, keepdims=True) + EPS) * g


def _dot(a, b):
    return jnp.dot(a, b, preferred_element_type=F32)


def _dot_nt(a, b):
    return lax.dot_general(a, b, (((1,), (1,)), ((), ())), preferred_element_type=F32)


def _rope_cols(c, cos, sin_lo, sin_hi, half):
    return c * cos + pltpu.roll(c, LANES - half, 1) * sin_lo + pltpu.roll(c, half, 1) * sin_hi


def _ada_kernel(c_ref, w_ref, b_ref, o_ref):
    c = c_ref[...]
    s = (c * jax.nn.sigmoid(c)).astype(BF16)
    o_ref[0] = _dot(s, w_ref[0].astype(BF16)) + b_ref[0]


def _ada(c_all, w_ada, b_ada):
    n = c_all.shape[0]
    tn = 1536
    return pl.pallas_call(
        _ada_kernel,
        grid=(DEPTH, 6 * D_MODEL // tn),
        in_specs=[pl.BlockSpec((n, D_MODEL), lambda i, j: (0, 0)),
                  pl.BlockSpec((1, D_MODEL, tn), lambda i, j: (i, 0, j)),
                  pl.BlockSpec((1, 1, tn), lambda i, j: (i, 0, j))],
        out_specs=pl.BlockSpec((1, n, tn), lambda i, j: (i, 0, j)),
        out_shape=jax.ShapeDtypeStruct((DEPTH, n, 6 * D_MODEL), F32),
        compiler_params=_params(("arbitrary", "arbitrary")),
        name="ada",
    )(c_all, w_ada, b_ada.reshape(DEPTH, 1, 6 * D_MODEL))


class _Path:
    def __init__(self, batch, seq):
        self.batch, self.seq = batch, seq
        self.n = batch * seq
        if seq >= TM:
            self.tm, self.tpb, self.mod_rows = TM, seq // TM, 1
        else:
            self.tm, self.tpb, self.mod_rows = self.n, 1, self.n
        self.steps = self.n // self.tm
        self.table_rows = self.tpb * self.tm

    def rows(self, width):
        return pl.BlockSpec((self.tm, width), lambda i: (i, 0))

    def mod(self):
        tpb = self.tpb
        return pl.BlockSpec((1, self.mod_rows, D_MODEL), lambda i: (i // tpb, 0, 0))

    def table(self):
        tpb = self.tpb
        return pl.BlockSpec((self.tm, LANES), lambda i: (i % tpb, 0))

    def expand_mod(self, m):
        if self.mod_rows == 1:
            return m[:, None, :]
        return jnp.repeat(m, self.seq, axis=0)[None]

    def expand_table(self, t):
        if self.table_rows == self.seq:
            return t
        return jnp.tile(t, (self.table_rows // self.seq, 1))


def _rope_tables(pos, rot, period):
    half = rot // 2
    inv = jnp.power(jnp.float32(ROPE_THETA), -jnp.arange(half, dtype=jnp.float32) * (2.0 / rot))
    ang = pos.astype(jnp.float32)[:, None] * inv[None, :]
    cos, sin = jnp.cos(ang), jnp.sin(ang)
    n = pos.shape[0]
    reps = LANES // period
    pad1 = jnp.ones((n, period - rot), F32)
    pad0 = jnp.zeros((n, period - rot), F32)
    zh = jnp.zeros((n, half), F32)
    cos_t = jnp.tile(jnp.concatenate([cos, cos, pad1], axis=1), (1, reps))
    sin_lo = jnp.tile(jnp.concatenate([-sin, zh, pad0], axis=1), (1, reps))
    sin_hi = jnp.tile(jnp.concatenate([zh, sin, pad0], axis=1), (1, reps))
    return cos_t, sin_lo, sin_hi


def _proj_a_kernel(*refs, transposed, aliased, tm):
    refs = refs[2:] if aliased else refs
    (x_ref, sh_ref, sc_ref, g_ref, w_ref, cos_ref, slo_ref, shi_ref,
     q_ref, k_ref, v_ref, krow_ref, vrow_ref) = refs
    h = (_rms(x_ref[...], g_ref[...]) * (1.0 + sc_ref[0]) + sh_ref[0]).astype(BF16)
    cos, slo, shi = cos_ref[...], slo_ref[...], shi_ref[...]
    q = _dot(h, w_ref[:, 0:D_MODEL])
    k = _dot(h, w_ref[:, D_MODEL:2 * D_MODEL])
    v = _dot(h, w_ref[:, 2 * D_MODEL:3 * D_MODEL])
    for j in range(H_A):
        cs = slice(j * LANES, (j + 1) * LANES)
        qj = _rope_cols(q[:, cs], cos, slo, shi, ROT_A // 2) * QSCALE_A
        kj = _rope_cols(k[:, cs], cos, slo, shi, ROT_A // 2)
        vj = v[:, cs]
        krow_ref[pl.ds(j, tm, stride=H_A), :] = kj
        vrow_ref[pl.ds(j, tm, stride=H_A), :] = vj
        k_ref[:, cs] = kj.astype(BF16)
        if transposed:
            qt, vt = qj.T.astype(BF16), vj.T.astype(BF16)
            for t in range(tm // TB):
                q_ref[0, t, cs, :] = qt[:, t * TB:(t + 1) * TB]
                v_ref[0, t, cs, :] = vt[:, t * TB:(t + 1) * TB]
        else:
            q_ref[:, cs] = qj.astype(BF16)
            v_ref[:, cs] = vj.astype(BF16)


def _proj_a(path, x, sh, sc, g, w_qkv, tables, layer_slot, n_slots, prev_rows):
    n, tm, tpb, steps = path.n, path.tm, path.tpb, path.steps
    transposed = path.seq >= TM
    aliased = prev_rows is not None
    bf = jax.ShapeDtypeStruct((n, D_MODEL), BF16)
    if transposed:
        bft = jax.ShapeDtypeStruct((path.batch, path.seq // TB, D_MODEL, TB), BF16)
        tspec = pl.BlockSpec((1, tm // TB, D_MODEL, TB), lambda i: (i // tpb, i % tpb, 0, 0))
        qv_shape, qv_spec = bft, tspec
    else:
        qv_shape, qv_spec = bf, path.rows(D_MODEL)
    rows_shape = jax.ShapeDtypeStruct((n_slots * n * H_A, LANES), F32)
    rows_spec = pl.BlockSpec((tm * H_A, LANES), lambda i: (layer_slot * steps + i, 0))
    in_specs = [path.rows(D_MODEL), path.mod(), path.mod(), _const((1, D_MODEL)),
                _const((D_MODEL, 3 * D_MODEL)), path.table(), path.table(), path.table()]
    args = [x, sh, sc, g, w_qkv, *tables]
    aliases = {}
    if aliased:
        in_specs = [pl.BlockSpec(memory_space=pl.ANY)] * 2 + in_specs
        args = list(prev_rows) + args
        aliases = {0: 3, 1: 4}
    return pl.pallas_call(
        functools.partial(_proj_a_kernel, transposed=transposed, aliased=aliased, tm=tm),
        grid=(steps,),
        in_specs=in_specs,
        out_specs=[qv_spec, path.rows(D_MODEL), qv_spec, rows_spec, rows_spec],
        out_shape=[qv_shape, bf, qv_shape, rows_shape, rows_shape],
        input_output_aliases=aliases,
        compiler_params=_params(("arbitrary",)),
        name="proj_a",
    )(*args)


def _lambda_full(lam_ref, lambda_init):
    lv = lam_ref[0]
    d1 = jnp.sum(lv[0:1] * lv[1:2], axis=-1, keepdims=True)
    d2 = jnp.sum(lv[2:3] * lv[3:4], axis=-1, keepdims=True)
    return jnp.exp(d1) - jnp.exp(d2) + lambda_init


def _split_subheads(q):
    lane = lax.broadcasted_iota(jnp.int32, q.shape, 1)
    zero = jnp.zeros_like(q)
    return [jnp.where(lane < D_SUB, q, zero), jnp.where(lane >= D_SUB, q, zero)]


def _diff_combine(o1, o2, lam, g_sub, lambda_init):
    o = o1 - lam * o2
    return _rms(o, g_sub) * (1.0 - lambda_init)


def _flash_kernel(*refs, diff, lambda_init, n_tiles, heads, dqk):
    if diff:
        qt_ref, k_ref, vt_ref, lam_ref, gsub_ref, o_ref, m_scr, l_scr, acc_scr, sa_scr, sb_scr = refs
    else:
        qt_ref, k_ref, vt_ref, o_ref, m_scr, l_scr, acc_scr, sa_scr, sb_scr = refs
    n_sub = 2 if diff else 1
    key = lax.broadcasted_iota(jnp.int32, (TB, TB), 0)
    qry = lax.broadcasted_iota(jnp.int32, (TB, TB), 1)
    visible = (key // CHUNK) <= (qry // CHUNK)

    def q_tile(i, carry):
        qs = []
        for h in range(heads):
            qt = qt_ref[0, i, h * dqk:(h + 1) * dqk, :]
            if diff:
                dim = lax.broadcasted_iota(jnp.int32, qt.shape, 0)
                zero = jnp.zeros_like(qt)
                qs += [jnp.where(dim < D_SUB, qt, zero), jnp.where(dim >= D_SUB, qt, zero)]
            else:
                qs.append(qt)
        m_scr[...] = jnp.full(m_scr.shape, NEG, F32)
        l_scr[...] = jnp.zeros(l_scr.shape, F32)
        acc_scr[...] = jnp.zeros(acc_scr.shape, F32)

        def scores(j, s_scr):
            rows = pl.ds(pl.multiple_of(j * TB, TB), TB)
            for h in range(heads):
                kb = k_ref[0, rows, h * dqk:(h + 1) * dqk]
                for s in range(n_sub):
                    s_scr[h * n_sub + s] = _dot(kb, qs[h * n_sub + s])

        def update(j, s_scr, masked):
            for h in range(heads):
                vb = vt_ref[0, j, h * LANES:(h + 1) * LANES, :]
                for s in range(n_sub):
                    c = h * n_sub + s
                    st = jnp.where(visible, s_scr[c], NEG) if masked else s_scr[c]
                    m_prev = m_scr[c]
                    m_new = jnp.maximum(m_prev, jnp.max(st, axis=0, keepdims=True))
                    alpha = jnp.exp2(m_prev - m_new)
                    p = jnp.exp2(st - m_new)
                    l_scr[c] = alpha * l_scr[c] + jnp.sum(p, axis=0, keepdims=True)
                    acc_scr[c] = alpha * acc_scr[c] + _dot(vb, p.astype(BF16))
                    m_scr[c] = m_new

        scores(0, sa_scr)

        def pair(t, c):
            scores(2 * t + 1, sb_scr)
            update(2 * t, sa_scr, False)
            scores(2 * t + 2, sa_scr)
            update(2 * t + 1, sb_scr, False)
            return c

        lax.fori_loop(0, i // 2, pair, 0)

        @pl.when(i % 2 == 0)
        def _():
            update(i, sa_scr, True)

        @pl.when(i % 2 == 1)
        def _():
            scores(i, sb_scr)
            update(i - 1, sa_scr, False)
            update(i, sb_scr, True)

        out_rows = pl.ds(pl.multiple_of(i * TB, TB), TB)
        for h in range(heads):
            if diff:
                c = 2 * h
                ot = acc_scr[c] / l_scr[c] - _lambda_full(lam_ref, lambda_init) * (acc_scr[c + 1] / l_scr[c + 1])
                ot = (ot * lax.rsqrt(jnp.mean(ot * ot, axis=0, keepdims=True) + EPS) * gsub_ref[0]
                      * (1.0 - lambda_init))
            else:
                ot = acc_scr[h] / l_scr[h]
            o_ref[0, out_rows, h * LANES:(h + 1) * LANES] = ot.T.astype(BF16)
        return carry

    lax.fori_loop(0, n_tiles, q_tile, 0)


def _flash(qt, k, vt, batch, seq, heads, dqk, diff_args=None, lambda_init=0.0):
    diff = diff_args is not None
    hg = HEAD_GROUP
    n_tiles = seq // TB
    chains = hg * (2 if diff else 1)
    in_specs = [pl.BlockSpec((1, n_tiles, hg * dqk, TB), lambda b, g: (b, 0, g, 0)),
                pl.BlockSpec((1, seq, hg * dqk), lambda b, g: (b, 0, g)),
                pl.BlockSpec((1, n_tiles, hg * LANES, TB), lambda b, g: (b, 0, g, 0))]
    args = [qt, k, vt]
    if diff:
        lam, g_sub_t, jl = diff_args
        in_specs += [pl.BlockSpec((1, 4, D_SUB), lambda b, g: (jl, 0, 0)),
                     pl.BlockSpec((1, LANES, TB), lambda b, g: (jl, 0, 0))]
        args += [lam, g_sub_t]
    return pl.pallas_call(
        functools.partial(_flash_kernel, diff=diff, lambda_init=lambda_init, n_tiles=n_tiles, heads=hg, dqk=dqk),
        grid=(batch, heads // hg),
        in_specs=in_specs,
        out_specs=pl.BlockSpec((1, seq, hg * LANES), lambda b, g: (b, 0, g)),
        out_shape=jax.ShapeDtypeStruct((batch, seq, heads * LANES), BF16),
        scratch_shapes=[pltpu.VMEM((chains, 1, TB), F32), pltpu.VMEM((chains, 1, TB), F32),
                        pltpu.VMEM((chains, LANES, TB), F32),
                        pltpu.VMEM((chains, TB, TB), F32), pltpu.VMEM((chains, TB, TB), F32)],
        compiler_params=_params(("arbitrary", "arbitrary")),
        name="flash_diff" if diff else "flash_mla",
    )(*args)


def _softmax_two(s_past, s_new):
    m = jnp.maximum(jnp.max(s_past, axis=-1, keepdims=True), jnp.max(s_new, axis=-1, keepdims=True))
    p_past = jnp.exp2(s_past - m)
    p_new = jnp.exp2(s_new - m)
    l = jnp.sum(p_past, axis=-1, keepdims=True) + jnp.sum(p_new, axis=-1, keepdims=True)
    return p_past, p_new, l


def _decode_a_kernel(q_ref, kn_ref, vn_ref, kp_ref, vp_ref, lam_ref, gsub_ref, o_ref, *, lambda_init, past, seq):
    lam = _lambda_full(lam_ref, lambda_init)
    q_all, kn_all, vn_all = q_ref[0], kn_ref[0], vn_ref[0]
    s_past, s_new = [], []
    for h in range(H_A):
        cs = slice(h * LANES, (h + 1) * LANES)
        q2 = jnp.concatenate(_split_subheads(q_all[:, cs]), axis=0)
        kp = kp_ref[pl.ds(h, past, stride=H_A), :].astype(BF16)
        s_past.append(_dot_nt(q2, kp))
        s_new.append(_dot_nt(q2, kn_all[:, cs]))
    for h in range(H_A):
        cs = slice(h * LANES, (h + 1) * LANES)
        vp = vp_ref[pl.ds(h, past, stride=H_A), :].astype(BF16)
        p_past, p_new, l = _softmax_two(s_past[h], s_new[h])
        o2 = (_dot(p_past.astype(BF16), vp) + _dot(p_new.astype(BF16), vn_all[:, cs])) / l
        o = _diff_combine(o2[0:seq], o2[seq:2 * seq], lam, gsub_ref[0], lambda_init)
        o_ref[0, :, cs] = o.astype(BF16)


def _decode_a(q, k, v, cache_k, cache_v, lam, g_sub, jl, lambda_init, batch, seq):
    past = cache_k.shape[0] // (H_A * batch * lam.shape[0])
    new = pl.BlockSpec((1, seq, D_MODEL), lambda b: (b, 0, 0))
    old = pl.BlockSpec((past * H_A, LANES), lambda b: (jl * batch + b, 0))
    return pl.pallas_call(
        functools.partial(_decode_a_kernel, lambda_init=lambda_init, past=past, seq=seq),
        grid=(batch,),
        in_specs=[new, new, new, old, old,
                  pl.BlockSpec((1, 4, D_SUB), lambda b: (jl, 0, 0)),
                  pl.BlockSpec((1, 1, LANES), lambda b: (jl, 0, 0))],
        out_specs=new,
        out_shape=jax.ShapeDtypeStruct((batch, seq, D_MODEL), BF16),
        compiler_params=_params(("arbitrary",)),
        name="decode_a",
    )(q, k, v, cache_k, cache_v, lam, g_sub)


def _proj_c_kernel(x_ref, sh_ref, sc_ref, g_ref, w1_ref, gq_ref, wuq_ref, gkv_ref, wukv_ref,
                   cos_ref, slo_ref, shi_ref, q_ref, k_ref, v_ref, kvrow_ref, krrow_ref, *, transposed, tm):
    h = (_rms(x_ref[...], g_ref[...]) * (1.0 + sc_ref[0]) + sh_ref[0]).astype(BF16)
    cos, slo, shi = cos_ref[...], slo_ref[...], shi_ref[...]
    d = _dot(h, w1_ref[...])
    qlat = _rms(d[:, 0:Q_LORA], gq_ref[...]).astype(BF16)
    kv_row = _rms(d[:, Q_LORA:Q_LORA + KV_LORA], gkv_ref[...])
    kr = _rope_cols(d[:, Q_LORA + KV_LORA:], cos, slo, shi, ROPE_C // 2)
    kvrow_ref[...] = kv_row
    krrow_ref[...] = kr[:, 0:ROPE_C]
    kr_bf = kr.astype(BF16)
    q = _dot(qlat, wuq_ref[...]) * QSCALE_C
    kv = _dot(kv_row.astype(BF16), wukv_ref[...])

    def put(ref, cs, val):
        if transposed:
            vt = val.T.astype(BF16)
            for t in range(tm // TB):
                ref[0, t, cs, :] = vt[:, t * TB:(t + 1) * TB]
        else:
            ref[:, cs] = val.astype(BF16)

    for hh in range(H_C):
        lo = slice(hh * DQK_C_PAD, hh * DQK_C_PAD + LANES)
        hi = slice(hh * DQK_C_PAD + LANES, (hh + 1) * DQK_C_PAD)
        put(q_ref, lo, q[:, lo])
        put(q_ref, hi, _rope_cols(q[:, hi], cos, slo, shi, ROPE_C // 2))
        k_ref[:, lo] = kv[:, lo].astype(BF16)
        k_ref[:, hi] = kr_bf
        put(v_ref, slice(hh * LANES, (hh + 1) * LANES), kv[:, hi])


def _proj_c(path, x, sh, sc, g, w1, g_q, w_uq, g_kv, w_ukv, tables):
    n, tm, tpb = path.n, path.tm, path.tpb
    w1n = w1.shape[1]
    transposed = path.seq >= TM
    if transposed:
        tiled = lambda width: (jax.ShapeDtypeStruct((path.batch, path.seq // TB, width, TB), BF16),
                               pl.BlockSpec((1, tm // TB, width, TB), lambda i: (i // tpb, i % tpb, 0, 0)))
    else:
        tiled = lambda width: (jax.ShapeDtypeStruct((n, width), BF16), path.rows(width))
    q_shape, q_spec = tiled(H_C * DQK_C_PAD)
    v_shape, v_spec = tiled(H_C * V_C)
    return pl.pallas_call(
        functools.partial(_proj_c_kernel, transposed=transposed, tm=tm),
        grid=(path.steps,),
        in_specs=[path.rows(D_MODEL), path.mod(), path.mod(), _const((1, D_MODEL)),
                  _const((D_MODEL, w1n)), _const((1, Q_LORA)), _const((Q_LORA, H_C * DQK_C_PAD)),
                  _const((1, KV_LORA)), _const((KV_LORA, H_C * (NOPE_C + V_C))),
                  path.table(), path.table(), path.table()],
        out_specs=[q_spec, path.rows(H_C * DQK_C_PAD), v_spec, path.rows(KV_LORA), path.rows(ROPE_C)],
        out_shape=[q_shape, jax.ShapeDtypeStruct((n, H_C * DQK_C_PAD), BF16), v_shape,
                   jax.ShapeDtypeStruct((n, KV_LORA), F32), jax.ShapeDtypeStruct((n, ROPE_C), F32)],
        compiler_params=_params(("arbitrary",)),
        name="proj_c",
    )(x, sh, sc, g, w1, g_q, w_uq, g_kv, w_ukv, *tables)


def _decode_c_kernel(q_ref, kn_ref, vn_ref, kvp_ref, krp_ref, wukv_ref, o_ref):
    q = q_ref[0]
    kvh = _dot(kvp_ref[0, 0].astype(BF16), wukv_ref[...])
    k_past = kvh[:, 0:NOPE_C].astype(BF16)
    v_past = kvh[:, NOPE_C:].astype(BF16)
    kr_past = krp_ref[0, 0].astype(BF16)
    s_past = _dot_nt(q[:, 0:NOPE_C], k_past) + _dot_nt(q[:, NOPE_C:NOPE_C + ROPE_C], kr_past)
    p_past, p_new, l = _softmax_two(s_past, _dot_nt(q, kn_ref[0]))
    o = (_dot(p_past.astype(BF16), v_past) + _dot(p_new.astype(BF16), vn_ref[0])) / l
    o_ref[0] = o.astype(BF16)


def _decode_c(q, k, v, cache_kv, cache_kr, w_ukv, jl, batch, seq):
    past = cache_kv.shape[2]
    return pl.pallas_call(
        _decode_c_kernel,
        grid=(batch, H_C),
        in_specs=[pl.BlockSpec((1, seq, DQK_C_PAD), lambda b, h: (b, 0, h)),
                  pl.BlockSpec((1, seq, DQK_C_PAD), lambda b, h: (b, 0, h)),
                  pl.BlockSpec((1, seq, LANES), lambda b, h: (b, 0, h)),
                  pl.BlockSpec((1, 1, past, KV_LORA), lambda b, h: (jl, b, 0, 0)),
                  pl.BlockSpec((1, 1, past, ROPE_C), lambda b, h: (jl, b, 0, 0)),
                  pl.BlockSpec((KV_LORA, NOPE_C + V_C), lambda b, h: (0, h))],
        out_specs=pl.BlockSpec((1, seq, LANES), lambda b, h: (b, 0, h)),
        out_shape=jax.ShapeDtypeStruct((batch, seq, H_C * V_C), BF16),
        compiler_params=_params(("arbitrary", "arbitrary")),
        name="decode_c",
    )(q, k, v, cache_kv, cache_kr, w_ukv)


def _proj_b_kernel(x_ref, sh_ref, sc_ref, g_ref, w_ref, bg_ref,
                   q_ref, k_ref, kt_ref, v_ref, opre_ref, gates_ref, gt_ref):
    nqk, nv = H_B * DQK_B, H_B * DV_B
    h = (_rms(x_ref[...], g_ref[...]) * (1.0 + sc_ref[0]) + sh_ref[0]).astype(BF16)
    q_ref[...] = _dot(h, w_ref[:, 0:nqk]).astype(BF16)
    k = _dot(h, w_ref[:, nqk:2 * nqk]) * (DQK_B ** -0.5)
    k_ref[...] = k
    kt_ref[0] = k.T
    v_ref[...] = _dot(h, w_ref[:, 2 * nqk:2 * nqk + nv]).astype(BF16)
    opre_ref[...] = _dot(h, w_ref[:, 2 * nqk + nv:2 * nqk + 2 * nv])
    gates = _dot(h, w_ref[:, 2 * nqk + 2 * nv:]) + bg_ref[...]
    gates_ref[...] = gates
    gt_ref[0] = gates.T[0:2 * H_B, :]


def _proj_b(path, x, sh, sc, g, w_in, b_gates):
    n, nqk, nv = path.n, H_B * DQK_B, H_B * DV_B
    groups = path.n // path.table_rows
    tpb = path.tpb
    cols = lambda width: pl.BlockSpec((1, width, path.tm), lambda i: (i // tpb, 0, i % tpb))
    return pl.pallas_call(
        _proj_b_kernel,
        grid=(path.steps,),
        in_specs=[path.rows(D_MODEL), path.mod(), path.mod(), _const((1, D_MODEL)),
                  _const((D_MODEL, w_in.shape[1])), _const((1, LANES))],
        out_specs=[path.rows(nqk), path.rows(nqk), cols(nqk), path.rows(nv), path.rows(nv),
                   path.rows(LANES), cols(2 * H_B)],
        out_shape=[jax.ShapeDtypeStruct((n, nqk), BF16), jax.ShapeDtypeStruct((n, nqk), F32),
                   jax.ShapeDtypeStruct((groups, nqk, path.table_rows), F32),
                   jax.ShapeDtypeStruct((n, nv), BF16), jax.ShapeDtypeStruct((n, nv), F32),
                   jax.ShapeDtypeStruct((n, LANES), F32),
                   jax.ShapeDtypeStruct((groups, 2 * H_B, path.table_rows), F32)],
        compiler_params=_params(("arbitrary",)),
        name="proj_b",
    )(x, sh, sc, g, w_in, b_gates)


def _mlstm_kernel(q_ref, k_ref, kt_ref, v_ref, g_ref, gt_ref, c0_ref, n0_ref, m0_ref,
                  hs_ref, ct_ref, nt_ref, mt_ref, c_scr, n_scr, m_scr, *, chunk, chunks_per_step):
    step = pl.program_id(1)

    @pl.when(step == 0)
    def _():
        c_scr[...] = c0_ref[0]
        n_scr[...] = n0_ref[0]
        m_scr[...] = m0_ref[0]

    L = chunk
    row = lax.broadcasted_iota(jnp.int32, (L, L), 0)
    col = lax.broadcasted_iota(jnp.int32, (L, L), 1)
    tril = col <= row
    triu = row <= col
    for c in range(chunks_per_step):
        sl = slice(c * L, (c + 1) * L)
        g_col = g_ref[0, sl, :]
        g_row = gt_ref[0, :, sl]
        lf_col = jax.nn.log_sigmoid(g_col)
        lf_row = jax.nn.log_sigmoid(g_row)
        for h in range(H_B):
            ig_col, f_col = g_col[:, h:h + 1], lf_col[:, H_B + h:H_B + h + 1]
            ig_row, f_row = g_row[h:h + 1, :], lf_row[H_B + h:H_B + h + 1, :]
            b_col = jnp.sum(jnp.where(tril, f_row, 0.0), axis=1, keepdims=True)
            b_row = jnp.sum(jnp.where(triu, f_col, 0.0), axis=0, keepdims=True)
            g_tot = jnp.sum(f_row, axis=1, keepdims=True)
            m_prev = m_scr[h]
            dm = jnp.where(tril, b_col - b_row + ig_row, NEG)
            inter = b_col + m_prev
            m_t = jnp.maximum(inter, jnp.max(dm, axis=1, keepdims=True))
            w_ts = jnp.exp(dm - m_t)
            a = jnp.exp(inter - m_t)
            qh = q_ref[0, sl, h * DQK_B:(h + 1) * DQK_B]
            kh = k_ref[0, sl, h * DQK_B:(h + 1) * DQK_B]
            vh = v_ref[0, sl, h * DV_B:(h + 1) * DV_B]
            c_prev = c_scr[h]
            n_prev = n_scr[h]
            qk = _dot_nt(qh, kh.astype(BF16)) * w_ts
            num = a * _dot(qh, c_prev.astype(BF16)) + _dot(qk.astype(BF16), vh)
            qn = jnp.sum(qh.astype(F32) * n_prev, axis=1, keepdims=True)
            den = a * qn + jnp.sum(qk, axis=1, keepdims=True)
            hs_ref[0, sl, h * DV_B:(h + 1) * DV_B] = num / jnp.maximum(jnp.abs(den), jnp.exp(-m_t))
            r_row = g_tot - b_row + ig_row
            r_col = g_tot - b_col + ig_col
            m_new = jnp.maximum(g_tot + m_prev, jnp.max(r_row, axis=1, keepdims=True))
            decay = jnp.exp(g_tot + m_prev - m_new)
            kw_t = kt_ref[0, h * DQK_B:(h + 1) * DQK_B, sl] * jnp.exp(r_row - m_new)
            c_scr[h] = decay * c_prev + _dot(kw_t.astype(BF16), vh)
            n_scr[h] = decay * n_prev + jnp.sum(kh * jnp.exp(r_col - m_new), axis=0, keepdims=True)
            m_scr[h] = m_new

    @pl.when(step == pl.num_programs(1) - 1)
    def _():
        ct_ref[0] = c_scr[...]
        nt_ref[0] = n_scr[...]
        mt_ref[0] = m_scr[...]


def _mlstm(q, k, kt, v, gates, gt, c0, n0, m0, batch, seq):
    chunk = CHUNK if seq % CHUNK == 0 else seq
    cps = max(1, min(LANES // chunk, seq // chunk))
    span = chunk * cps
    nqk, nv = H_B * DQK_B, H_B * DV_B
    rows = lambda width: pl.BlockSpec((1, span, width), lambda b, s: (b, s, 0))
    cols = lambda height: pl.BlockSpec((1, height, span), lambda b, s: (b, 0, s))
    state = lambda *shape: pl.BlockSpec((1,) + shape, lambda b, s: (b,) + (0,) * len(shape))
    return pl.pallas_call(
        functools.partial(_mlstm_kernel, chunk=chunk, chunks_per_step=cps),
        grid=(batch, seq // span),
        in_specs=[rows(nqk), rows(nqk), cols(nqk), rows(nv), rows(LANES), cols(2 * H_B),
                  state(H_B, DQK_B, DV_B), state(H_B, 1, DQK_B), state(H_B, 1, 1)],
        out_specs=[rows(nv), state(H_B, DQK_B, DV_B), state(H_B, 1, DQK_B), state(H_B, 1, 1)],
        out_shape=[jax.ShapeDtypeStruct((batch, seq, nv), F32),
                   jax.ShapeDtypeStruct((batch, H_B, DQK_B, DV_B), F32),
                   jax.ShapeDtypeStruct((batch, H_B, 1, DQK_B), F32),
                   jax.ShapeDtypeStruct((batch, H_B, 1, 1), F32)],
        scratch_shapes=[pltpu.VMEM((H_B, DQK_B, DV_B), F32), pltpu.VMEM((H_B, 1, DQK_B), F32),
                        pltpu.VMEM((H_B, 1, 1), F32)],
        compiler_params=_params(("arbitrary", "arbitrary")),
        name="mlstm",
    )(q, k, kt, v, gates, gt, c0, n0, m0)


def _out_kernel(o_ref, x_ref, gt_ref, w_ref, y_ref):
    y_ref[...] = x_ref[...] + gt_ref[0] * _dot(o_ref[...], w_ref[...])


def _out_b_kernel(hs_ref, opre_ref, gout_ref, x_ref, gt_ref, w_ref, y_ref):
    hs = hs_ref[...]
    gout = gout_ref[...]
    cols = []
    for h in range(H_B):
        cs = slice(h * DV_B, (h + 1) * DV_B)
        cols.append(_rms(hs[:, cs], gout[:, cs]))
    z = (jax.nn.sigmoid(opre_ref[...]) * jnp.concatenate(cols, axis=-1)).astype(BF16)
    y_ref[...] = x_ref[...] + gt_ref[0] * _dot(z, w_ref[...])


def _out_proj(path, o, x, gt, w):
    return pl.pallas_call(
        _out_kernel,
        grid=(path.steps,),
        in_specs=[path.rows(D_MODEL), path.rows(D_MODEL), path.mod(), _const((D_MODEL, D_MODEL))],
        out_specs=path.rows(D_MODEL),
        out_shape=jax.ShapeDtypeStruct((path.n, D_MODEL), F32),
        compiler_params=_params(("arbitrary",)),
        name="out_proj",
    )(o, x, gt, w)


def _out_proj_b(path, hs, opre, g_out, x, gt, w):
    return pl.pallas_call(
        _out_b_kernel,
        grid=(path.steps,),
        in_specs=[path.rows(D_MODEL), path.rows(D_MODEL), _const((1, D_MODEL)), path.rows(D_MODEL),
                  path.mod(), _const((D_MODEL, D_MODEL))],
        out_specs=path.rows(D_MODEL),
        out_shape=jax.ShapeDtypeStruct((path.n, D_MODEL), F32),
        compiler_params=_params(("arbitrary",)),
        name="out_proj_b",
    )(hs, opre, g_out, x, gt, w)


def _ffn_kernel(x_ref, sh_ref, sc_ref, gt_ref, g_ref, win_ref, wout_ref, gfin_ref, y_ref, acc_scr, *, final):
    x = x_ref[...]
    h = (_rms(x, g_ref[...]) * (1.0 + sc_ref[0]) + sh_ref[0]).astype(BF16)
    for c in range(D_FF // FF_CHUNK):
        a = _dot(h, win_ref[:, c * FF_CHUNK:(c + 1) * FF_CHUNK])
        b = _dot(h, win_ref[:, D_FF + c * FF_CHUNK:D_FF + (c + 1) * FF_CHUNK])
        act = (a * jax.nn.sigmoid(a) * b).astype(BF16)
        part = _dot(act, wout_ref[c * FF_CHUNK:(c + 1) * FF_CHUNK, :])
        if c == 0:
            acc_scr[...] = part
        else:
            acc_scr[...] += part
    y = x + gt_ref[0] * acc_scr[...]
    y_ref[...] = _rms(y, gfin_ref[...]) if final else y


def _ffn(path, x, sh, sc, gt, g, w_in, w_out, g_final, final):
    return pl.pallas_call(
        functools.partial(_ffn_kernel, final=final),
        grid=(path.steps,),
        in_specs=[path.rows(D_MODEL), path.mod(), path.mod(), path.mod(), _const((1, D_MODEL)),
                  _const((D_MODEL, 2 * D_FF)), _const((D_FF, D_MODEL)), _const((1, D_MODEL))],
        out_specs=path.rows(D_MODEL),
        out_shape=jax.ShapeDtypeStruct((path.n, D_MODEL), F32),
        scratch_shapes=[pltpu.VMEM((path.tm, D_MODEL), F32)],
        compiler_params=_params(("arbitrary",)),
        name="ffn",
    )(x, sh, sc, gt, g, w_in, w_out, g_final)


def _prep_weights(p):
    w = {}
    w['a_qkv'] = p['w_a_qkv'].astype(BF16)
    w['a_o'] = p['w_a_o'].astype(BF16)
    nqk, nv = H_B * DQK_B, H_B * DV_B
    w_b = p['w_b_in']
    gate_pad = jnp.zeros(w_b.shape[:2] + (LANES - 2 * H_B,), w_b.dtype)
    w['b_in'] = jnp.concatenate([w_b, gate_pad], axis=-1).astype(BF16)
    bg = p['b_b_gates']
    w['b_gates'] = jnp.concatenate([bg, jnp.zeros((bg.shape[0], LANES - 2 * H_B), bg.dtype)], axis=-1)[:, None, :]
    w['b_out'] = p['w_b_out'].astype(BF16)
    n_c = p['w_c_dq'].shape[0]
    w_dkv = p['w_c_dkv']
    w['c_1'] = jnp.concatenate([p['w_c_dq'], w_dkv, jnp.zeros((n_c, D_MODEL, LANES - ROPE_C), w_dkv.dtype)],
                               axis=-1).astype(BF16)
    w_uq = p['w_c_uq'].reshape(n_c, Q_LORA, H_C, NOPE_C + ROPE_C)
    w_uq = jnp.concatenate([w_uq, jnp.zeros((n_c, Q_LORA, H_C, DQK_C_PAD - NOPE_C - ROPE_C), w_uq.dtype)], axis=-1)
    w['c_uq'] = w_uq.reshape(n_c, Q_LORA, H_C * DQK_C_PAD).astype(BF16)
    w['c_ukv'] = p['w_c_ukv'].astype(BF16)
    w['c_o'] = p['w_c_o'].astype(BF16)
    w['ffn_in'] = p['w_ffn_in'].astype(BF16)
    w['ffn_out'] = p['w_ffn_out'].astype(BF16)
    return w


def _trunk(x, mods, past, p, w):
    batch, seq, _ = x.shape
    path = _Path(batch, seq)
    n_past = 0 if past is None else past[5].shape[2]
    pos = n_past + jnp.arange(seq, dtype=jnp.int32)
    tab_a = tuple(path.expand_table(t) for t in _rope_tables(pos, ROT_A, D_SUB))
    tab_c = tuple(path.expand_table(t) for t in _rope_tables(pos, ROPE_C, LANES))
    x = x.reshape(path.n, D_MODEL)
    g_final = p['g_final'][None, :]
    n_a = (DEPTH + N_MIXERS - 1) // N_MIXERS
    a_rows = None
    b_c, b_n, b_m, c_kv, c_kr = [], [], [], [], []
    for i in range(DEPTH):
        kind, j = i % N_MIXERS, i // N_MIXERS
        sh1, sc1, gt1, sh2, sc2, gt2 = (path.expand_mod(m) for m in jnp.split(mods[i], 6, axis=-1))
        g1 = p['g_norm1'][i][None, :]
        if kind == 0:
            lambda_init = 0.8 - 0.6 * math.exp(-0.3 * i)
            q, k, v, k_row, v_row = _proj_a(path, x, sh1, sc1, g1, w['a_qkv'][j], tab_a, j, n_a, a_rows)
            a_rows = (k_row, v_row)
            shape3 = (batch, seq, D_MODEL)
            if past is None:
                g_sub_t = jnp.broadcast_to(p['g_a_sub'][:, :, None], p['g_a_sub'].shape + (TB,))
                o = _flash(q, k.reshape(shape3), v, batch, seq, H_A, LANES,
                           diff_args=(p['a_lambda'], g_sub_t, j), lambda_init=lambda_init)
            else:
                o = _decode_a(q.reshape(shape3), k.reshape(shape3), v.reshape(shape3), past[0], past[1],
                              p['a_lambda'], p['g_a_sub'][:, None, :], j, lambda_init, batch, seq)
            x = _out_proj(path, o.reshape(path.n, D_MODEL), x, gt1, w['a_o'][j])
        elif kind == 1:
            q, k, kt, v, opre, gates, gt = _proj_b(path, x, sh1, sc1, g1, w['b_in'][j], w['b_gates'][j])
            nqk, nv = H_B * DQK_B, H_B * DV_B
            if path.table_rows != seq:
                kt = kt.reshape(nqk, batch, seq).transpose(1, 0, 2)
                gt = gt.reshape(2 * H_B, batch, seq).transpose(1, 0, 2)
            if past is None:
                c0 = jnp.zeros((batch, H_B, DQK_B, DV_B), F32)
                n0 = jnp.zeros((batch, H_B, 1, DQK_B), F32)
                m0 = jnp.zeros((batch, H_B, 1, 1), F32)
            else:
                c0 = past[2][j]
                n0 = past[3][j][:, :, None, :]
                m0 = past[4][j][:, :, None, None]
            hs, c_t, n_t, m_t = _mlstm(q.reshape(batch, seq, nqk), k.reshape(batch, seq, nqk), kt,
                                       v.reshape(batch, seq, nv), gates.reshape(batch, seq, LANES), gt,
                                       c0, n0, m0, batch, seq)
            b_c.append(c_t)
            b_n.append(n_t[:, :, 0, :])
            b_m.append(m_t[:, :, 0, 0])
            x = _out_proj_b(path, hs.reshape(path.n, nv), opre, p['g_b_out'][j][None, :], x, gt1, w['b_out'][j])
        else:
            q, k, v, kv_row, kr_row = _proj_c(path, x, sh1, sc1, g1, w['c_1'][j], p['g_c_q'][j][None, :],
                                              w['c_uq'][j], p['g_c_kv'][j][None, :], w['c_ukv'][j], tab_c)
            qk_shape = (batch, seq, H_C * DQK_C_PAD)
            v_shape = (batch, seq, H_C * V_C)
            if past is None:
                o = _flash(q, k.reshape(qk_shape), v, batch, seq, H_C, DQK_C_PAD)
            else:
                o = _decode_c(q.reshape(qk_shape), k.reshape(qk_shape), v.reshape(v_shape), past[5], past[6],
                              w['c_ukv'][j], j, batch, seq)
            c_kv.append(kv_row.reshape(batch, seq, KV_LORA))
            c_kr.append(kr_row.reshape(batch, seq, ROPE_C))
            x = _out_proj(path, o.reshape(path.n, D_MODEL), x, gt1, w['c_o'][j])
        x = _ffn(path, x, sh2, sc2, gt2, p['g_norm2'][i][None, :], w['ffn_in'][i], w['ffn_out'][i],
                 g_final, final=(i == DEPTH - 1))
    y = x.reshape(batch, seq, D_MODEL)
    a_k, a_v = (r.reshape(n_a, batch, seq, H_A, 2 * D_SUB) for r in a_rows)
    return y, (a_k, a_v, jnp.stack(b_c), jnp.stack(b_n), jnp.stack(b_m), jnp.stack(c_kv), jnp.stack(c_kr))


def kernel(x_prompt, x_sample, c_prompt, c_sample, cache_a_k, cache_a_v, state_b_c, state_b_n, state_b_m, cache_c_kv, cache_c_kr, w_ada, b_ada, g_norm1, g_norm2, w_a_qkv, a_lambda, g_a_sub, w_a_o, w_b_in, b_b_gates, g_b_out, w_b_out, w_c_dq, g_c_q, w_c_uq, w_c_dkv, g_c_kv, w_c_ukv, w_c_o, w_ffn_in, w_ffn_out, g_final):
    p = dict(g_norm1=g_norm1, g_norm2=g_norm2, w_a_qkv=w_a_qkv, a_lambda=a_lambda, g_a_sub=g_a_sub,
             w_a_o=w_a_o, w_b_in=w_b_in, b_b_gates=b_b_gates, g_b_out=g_b_out, w_b_out=w_b_out,
             w_c_dq=w_c_dq, g_c_q=g_c_q, w_c_uq=w_c_uq, w_c_dkv=w_c_dkv, g_c_kv=g_c_kv, w_c_ukv=w_c_ukv,
             w_c_o=w_c_o, w_ffn_in=w_ffn_in, w_ffn_out=w_ffn_out, g_final=g_final)
    w = _prep_weights(p)
    nb_p = x_prompt.shape[0]
    mods = _ada(jnp.concatenate([c_prompt, c_sample], axis=0), w_ada, b_ada)
    y_prompt, sp = _trunk(x_prompt, mods[:, :nb_p], None, p, w)
    past = (cache_a_k.reshape(-1, 2 * D_SUB), cache_a_v.reshape(-1, 2 * D_SUB),
            state_b_c, state_b_n, state_b_m, cache_c_kv, cache_c_kr)
    y_sample, ss = _trunk(x_sample, mods[:, nb_p:], past, p, w)
    return (y_prompt, y_sample) + sp + ss
```

```python
import functools
import math

import jax
import jax.numpy as jnp
from jax import lax
from jax.experimental import pallas as pl
from jax.experimental.pallas import tpu as pltpu

F32 = jnp.float32
BF16 = jnp.bfloat16

D_MODEL = 1024
DEPTH = 4
CHUNK = 64
N_MIXERS = 3
ROPE_THETA = 500000.0
EPS = 1e-6
D_SUB = 64
H_A = D_MODEL // (2 * D_SUB)
ROT_A = D_SUB // 4
H_B = 4
DQK_B = D_MODEL // (2 * H_B)
DV_B = D_MODEL // H_B
H_C = 8
NOPE_C = 128
ROPE_C = 64
V_C = 128
Q_LORA = 384
KV_LORA = 256
D_FF = -(-8 * D_MODEL // (3 * 256)) * 256

LANES = 128
DQK_C_PAD = 2 * LANES
NEG = -1e30
VMEM_LIMIT = 56 * 1024 * 1024
TM = 512
TB = 256
HEAD_GROUP = 4
FF_CHUNK = 256
MLSTM_SPAN = 256
LOG2E = math.log2(math.e)
QSCALE_A = D_SUB ** -0.5 * LOG2E
QSCALE_C = (NOPE_C + ROPE_C) ** -0.5 * LOG2E


def _params(sem):
    return pltpu.CompilerParams(dimension_semantics=sem, vmem_limit_bytes=VMEM_LIMIT)


def _const(shape):
    nd = len(shape)
    return pl.BlockSpec(shape, lambda *_: (0,) * nd, pipeline_mode=pl.Buffered(1))


def _rms(x, g):
    return x * lax.rsqrt(jnp.mean(x * x, axis=-1, keepdims=True) + EPS) * g


def _dot(a, b):
    return jnp.dot(a, b, preferred_element_type=F32)


def _dot_nt(a, b):
    return lax.dot_general(a, b, (((1,), (1,)), ((), ())), preferred_element_type=F32)


def _rope_cols(c, cos, sin_lo, sin_hi, half):
    return c * cos + pltpu.roll(c, LANES - half, 1) * sin_lo + pltpu.roll(c, half, 1) * sin_hi


def _ada_kernel(c_ref, w_ref, b_ref, o_ref):
    c = c_ref[...]
    s = (c * jax.nn.sigmoid(c)).astype(BF16)
    o_ref[0] = _dot(s, w_ref[0].astype(BF16)) + b_ref[0]


def _ada(c_all, w_ada, b_ada):
    n = c_all.shape[0]
    tn = 1536
    return pl.pallas_call(
        _ada_kernel,
        grid=(DEPTH, 6 * D_MODEL // tn),
        in_specs=[pl.BlockSpec((n, D_MODEL), lambda i, j: (0, 0)),
                  pl.BlockSpec((1, D_MODEL, tn), lambda i, j: (i, 0, j)),
                  pl.BlockSpec((1, 1, tn), lambda i, j: (i, 0, j))],
        out_specs=pl.BlockSpec((1, n, tn), lambda i, j: (i, 0, j)),
        out_shape=jax.ShapeDtypeStruct((DEPTH, n, 6 * D_MODEL), F32),
        compiler_params=_params(("arbitrary", "arbitrary")),
        name="ada",
    )(c_all, w_ada, b_ada.reshape(DEPTH, 1, 6 * D_MODEL))


class _Path:
    def __init__(self, batch, seq):
        self.batch, self.seq = batch, seq
        self.n = batch * seq
        if seq >= TM:
            self.tm, self.tpb, self.mod_rows = TM, seq // TM, 1
        else:
            self.tm, self.tpb, self.mod_rows = self.n, 1, self.n
        self.steps = self.n // self.tm
        self.table_rows = self.tpb * self.tm

    def rows(self, width):
        return pl.BlockSpec((self.tm, width), lambda i: (i, 0))

    def mod(self):
        tpb = self.tpb
        return pl.BlockSpec((1, self.mod_rows, D_MODEL), lambda i: (i // tpb, 0, 0))

    def table(self):
        tpb = self.tpb
        return pl.BlockSpec((self.tm, LANES), lambda i: (i % tpb, 0))

    def expand_mod(self, m):
        if self.mod_rows == 1:
            return m[:, None, :]
        return jnp.repeat(m, self.seq, axis=0)[None]

    def expand_table(self, t):
        if self.table_rows == self.seq:
            return t
        return jnp.tile(t, (self.table_rows // self.seq, 1))


def _rope_tables(pos, rot, period):
    half = rot // 2
    inv = jnp.power(jnp.float32(ROPE_THETA), -jnp.arange(half, dtype=jnp.float32) * (2.0 / rot))
    ang = pos.astype(jnp.float32)[:, None] * inv[None, :]
    cos, sin = jnp.cos(ang), jnp.sin(ang)
    n = pos.shape[0]
    reps = LANES // period
    pad1 = jnp.ones((n, period - rot), F32)
    pad0 = jnp.zeros((n, period - rot), F32)
    zh = jnp.zeros((n, half), F32)
    cos_t = jnp.tile(jnp.concatenate([cos, cos, pad1], axis=1), (1, reps))
    sin_lo = jnp.tile(jnp.concatenate([-sin, zh, pad0], axis=1), (1, reps))
    sin_hi = jnp.tile(jnp.concatenate([zh, sin, pad0], axis=1), (1, reps))
    return cos_t, sin_lo, sin_hi


def _proj_a_kernel(*refs, transposed, aliased, tm, slot):
    refs = refs[2:] if aliased else refs
    (x_ref, sh_ref, sc_ref, g_ref, w_ref, cos_ref, slo_ref, shi_ref,
     q_ref, k_ref, v_ref, krow_all, vrow_all) = refs
    if aliased:
        krow_ref, vrow_ref = krow_all.at[0], vrow_all.at[0]
    else:
        krow_ref, vrow_ref = krow_all.at[slot], vrow_all.at[slot]
        for other in range(krow_all.shape[0]):
            if other != slot:
                krow_all[other] = jnp.zeros(krow_all.shape[1:], F32)
                vrow_all[other] = jnp.zeros(vrow_all.shape[1:], F32)
    h = (_rms(x_ref[...], g_ref[...]) * (1.0 + sc_ref[0]) + sh_ref[0]).astype(BF16)
    cos, slo, shi = cos_ref[...], slo_ref[...], shi_ref[...]
    q = _dot(h, w_ref[:, 0:D_MODEL])
    k = _dot(h, w_ref[:, D_MODEL:2 * D_MODEL])
    v = _dot(h, w_ref[:, 2 * D_MODEL:3 * D_MODEL])
    for j in range(H_A):
        cs = slice(j * LANES, (j + 1) * LANES)
        qj = _rope_cols(q[:, cs], cos, slo, shi, ROT_A // 2) * QSCALE_A
        kj = _rope_cols(k[:, cs], cos, slo, shi, ROT_A // 2)
        vj = v[:, cs]
        krow_ref[pl.ds(j, tm, stride=H_A), :] = kj
        vrow_ref[pl.ds(j, tm, stride=H_A), :] = vj
        k_ref[:, cs] = kj.astype(BF16)
        if transposed:
            qt, vt = qj.T.astype(BF16), vj.T.astype(BF16)
            for t in range(tm // TB):
                q_ref[0, t, cs, :] = qt[:, t * TB:(t + 1) * TB]
                v_ref[0, t, cs, :] = vt[:, t * TB:(t + 1) * TB]
        else:
            q_ref[:, cs] = qj.astype(BF16)
            v_ref[:, cs] = vj.astype(BF16)


def _proj_a(path, x, sh, sc, g, w_qkv, tables, layer_slot, n_slots, prev_rows):
    n, tm, tpb, steps = path.n, path.tm, path.tpb, path.steps
    transposed = path.seq >= TM
    aliased = prev_rows is not None
    bf = jax.ShapeDtypeStruct((n, D_MODEL), BF16)
    if transposed:
        bft = jax.ShapeDtypeStruct((path.batch, path.seq // TB, D_MODEL, TB), BF16)
        tspec = pl.BlockSpec((1, tm // TB, D_MODEL, TB), lambda i: (i // tpb, i % tpb, 0, 0))
        qv_shape, qv_spec = bft, tspec
    else:
        qv_shape, qv_spec = bf, path.rows(D_MODEL)
    rows_shape = jax.ShapeDtypeStruct((n_slots, n * H_A, LANES), F32)
    if aliased:
        rows_spec = pl.BlockSpec((1, tm * H_A, LANES), lambda i: (layer_slot, i, 0))
    else:
        rows_spec = pl.BlockSpec((n_slots, tm * H_A, LANES), lambda i: (0, i, 0))
    in_specs = [path.rows(D_MODEL), path.mod(), path.mod(), _const((1, D_MODEL)),
                _const((D_MODEL, 3 * D_MODEL)), path.table(), path.table(), path.table()]
    args = [x, sh, sc, g, w_qkv, *tables]
    aliases = {}
    if aliased:
        in_specs = [pl.BlockSpec(memory_space=pl.ANY)] * 2 + in_specs
        args = list(prev_rows) + args
        aliases = {0: 3, 1: 4}
    return pl.pallas_call(
        functools.partial(_proj_a_kernel, transposed=transposed, aliased=aliased, tm=tm, slot=layer_slot),
        grid=(steps,),
        in_specs=in_specs,
        out_specs=[qv_spec, path.rows(D_MODEL), qv_spec, rows_spec, rows_spec],
        out_shape=[qv_shape, bf, qv_shape, rows_shape, rows_shape],
        input_output_aliases=aliases,
        compiler_params=_params(("arbitrary",)),
        name="proj_a",
    )(*args)


def _lambda_full(lam_ref, lambda_init):
    lv = lam_ref[0]
    d1 = jnp.sum(lv[0:1] * lv[1:2], axis=-1, keepdims=True)
    d2 = jnp.sum(lv[2:3] * lv[3:4], axis=-1, keepdims=True)
    return jnp.exp(d1) - jnp.exp(d2) + lambda_init


def _split_subheads(q):
    lane = lax.broadcasted_iota(jnp.int32, q.shape, 1)
    zero = jnp.zeros_like(q)
    return [jnp.where(lane < D_SUB, q, zero), jnp.where(lane >= D_SUB, q, zero)]


def _diff_combine(o1, o2, lam, g_sub, lambda_init):
    o = o1 - lam * o2
    return _rms(o, g_sub) * (1.0 - lambda_init)


def _flash_kernel(*refs, diff, lambda_init, n_tiles, heads, dqk):
    if diff:
        qt_ref, k_ref, vt_ref, lam_ref, gsub_ref, o_ref, m_scr, l_scr, acc_scr, sa_scr, sb_scr = refs
    else:
        qt_ref, k_ref, vt_ref, o_ref, m_scr, l_scr, acc_scr, sa_scr, sb_scr = refs
    n_sub = 2 if diff else 1
    key = lax.broadcasted_iota(jnp.int32, (TB, TB), 0)
    qry = lax.broadcasted_iota(jnp.int32, (TB, TB), 1)
    visible = (key // CHUNK) <= (qry // CHUNK)

    def q_tile(i, carry):
        qs = []
        for h in range(heads):
            qt = qt_ref[0, i, h * dqk:(h + 1) * dqk, :]
            if diff:
                dim = lax.broadcasted_iota(jnp.int32, qt.shape, 0)
                zero = jnp.zeros_like(qt)
                qs += [jnp.where(dim < D_SUB, qt, zero), jnp.where(dim >= D_SUB, qt, zero)]
            else:
                qs.append(qt)
        m_scr[...] = jnp.full(m_scr.shape, NEG, F32)
        l_scr[...] = jnp.zeros(l_scr.shape, F32)
        acc_scr[...] = jnp.zeros(acc_scr.shape, F32)

        def scores(j, s_scr):
            rows = pl.ds(pl.multiple_of(j * TB, TB), TB)
            for h in range(heads):
                kb = k_ref[0, rows, h * dqk:(h + 1) * dqk]
                for s in range(n_sub):
                    s_scr[h * n_sub + s] = _dot(kb, qs[h * n_sub + s])

        def update(j, s_scr, masked):
            for h in range(heads):
                vb = vt_ref[0, j, h * LANES:(h + 1) * LANES, :]
                for s in range(n_sub):
                    c = h * n_sub + s
                    st = jnp.where(visible, s_scr[c], NEG) if masked else s_scr[c]
                    m_prev = m_scr[c]
                    m_new = jnp.maximum(m_prev, jnp.max(st, axis=0, keepdims=True))
                    alpha = jnp.exp2(m_prev - m_new)
                    p = jnp.exp2(st - m_new)
                    l_scr[c] = alpha * l_scr[c] + jnp.sum(p, axis=0, keepdims=True)
                    acc_scr[c] = alpha * acc_scr[c] + _dot(vb, p.astype(BF16))
                    m_scr[c] = m_new

        scores(0, sa_scr)

        def pair(t, c):
            scores(2 * t + 1, sb_scr)
            update(2 * t, sa_scr, False)
            scores(2 * t + 2, sa_scr)
            update(2 * t + 1, sb_scr, False)
            return c

        lax.fori_loop(0, i // 2, pair, 0)

        @pl.when(i % 2 == 0)
        def _():
            update(i, sa_scr, True)

        @pl.when(i % 2 == 1)
        def _():
            scores(i, sb_scr)
            update(i - 1, sa_scr, False)
            update(i, sb_scr, True)

        out_rows = pl.ds(pl.multiple_of(i * TB, TB), TB)
        for h in range(heads):
            if diff:
                c = 2 * h
                ot = acc_scr[c] / l_scr[c] - _lambda_full(lam_ref, lambda_init) * (acc_scr[c + 1] / l_scr[c + 1])
                ot = (ot * lax.rsqrt(jnp.mean(ot * ot, axis=0, keepdims=True) + EPS) * gsub_ref[0]
                      * (1.0 - lambda_init))
            else:
                ot = acc_scr[h] / l_scr[h]
            o_ref[0, out_rows, h * LANES:(h + 1) * LANES] = ot.T.astype(BF16)
        return carry

    lax.fori_loop(0, n_tiles, q_tile, 0)


def _flash(qt, k, vt, batch, seq, heads, dqk, diff_args=None, lambda_init=0.0):
    diff = diff_args is not None
    hg = HEAD_GROUP
    n_tiles = seq // TB
    chains = hg * (2 if diff else 1)
    in_specs = [pl.BlockSpec((1, n_tiles, hg * dqk, TB), lambda b, g: (b, 0, g, 0)),
                pl.BlockSpec((1, seq, hg * dqk), lambda b, g: (b, 0, g)),
                pl.BlockSpec((1, n_tiles, hg * LANES, TB), lambda b, g: (b, 0, g, 0))]
    args = [qt, k, vt]
    if diff:
        lam, g_sub_t, jl = diff_args
        in_specs += [pl.BlockSpec((1, 4, D_SUB), lambda b, g: (jl, 0, 0)),
                     pl.BlockSpec((1, LANES, TB), lambda b, g: (jl, 0, 0))]
        args += [lam, g_sub_t]
    return pl.pallas_call(
        functools.partial(_flash_kernel, diff=diff, lambda_init=lambda_init, n_tiles=n_tiles, heads=hg, dqk=dqk),
        grid=(batch, heads // hg),
        in_specs=in_specs,
        out_specs=pl.BlockSpec((1, seq, hg * LANES), lambda b, g: (b, 0, g)),
        out_shape=jax.ShapeDtypeStruct((batch, seq, heads * LANES), BF16),
        scratch_shapes=[pltpu.VMEM((chains, 1, TB), F32), pltpu.VMEM((chains, 1, TB), F32),
                        pltpu.VMEM((chains, LANES, TB), F32),
                        pltpu.VMEM((chains, TB, TB), F32), pltpu.VMEM((chains, TB, TB), F32)],
        compiler_params=_params(("arbitrary", "arbitrary")),
        name="flash_diff" if diff else "flash_mla",
    )(*args)


def _softmax_two(s_past, s_new):
    m = jnp.maximum(jnp.max(s_past, axis=-1, keepdims=True), jnp.max(s_new, axis=-1, keepdims=True))
    p_past = jnp.exp2(s_past - m)
    p_new = jnp.exp2(s_new - m)
    l = jnp.sum(p_past, axis=-1, keepdims=True) + jnp.sum(p_new, axis=-1, keepdims=True)
    return p_past, p_new, l


def _decode_a_kernel(q_ref, kn_ref, vn_ref, kp_ref, vp_ref, lam_ref, gsub_ref, o_ref, *, lambda_init, past, seq):
    lam = _lambda_full(lam_ref, lambda_init)
    q_all, kn_all, vn_all = q_ref[0], kn_ref[0], vn_ref[0]
    s_past, s_new = [], []
    for h in range(H_A):
        cs = slice(h * LANES, (h + 1) * LANES)
        q2 = jnp.concatenate(_split_subheads(q_all[:, cs]), axis=0)
        kp = kp_ref[pl.ds(h, past, stride=H_A), :].astype(BF16)
        s_past.append(_dot_nt(q2, kp))
        s_new.append(_dot_nt(q2, kn_all[:, cs]))
    for h in range(H_A):
        cs = slice(h * LANES, (h + 1) * LANES)
        vp = vp_ref[pl.ds(h, past, stride=H_A), :].astype(BF16)
        p_past, p_new, l = _softmax_two(s_past[h], s_new[h])
        o2 = (_dot(p_past.astype(BF16), vp) + _dot(p_new.astype(BF16), vn_all[:, cs])) / l
        o = _diff_combine(o2[0:seq], o2[seq:2 * seq], lam, gsub_ref[0], lambda_init)
        o_ref[0, :, cs] = o.astype(BF16)


def _decode_a(q, k, v, cache_k, cache_v, lam, g_sub, jl, lambda_init, batch, seq):
    past = cache_k.shape[0] // (H_A * batch * lam.shape[0])
    new = pl.BlockSpec((1, seq, D_MODEL), lambda b: (b, 0, 0))
    old = pl.BlockSpec((past * H_A, LANES), lambda b: (jl * batch + b, 0))
    return pl.pallas_call(
        functools.partial(_decode_a_kernel, lambda_init=lambda_init, past=past, seq=seq),
        grid=(batch,),
        in_specs=[new, new, new, old, old,
                  pl.BlockSpec((1, 4, D_SUB), lambda b: (jl, 0, 0)),
                  pl.BlockSpec((1, 1, LANES), lambda b: (jl, 0, 0))],
        out_specs=new,
        out_shape=jax.ShapeDtypeStruct((batch, seq, D_MODEL), BF16),
        compiler_params=_params(("arbitrary",)),
        name="decode_a",
    )(q, k, v, cache_k, cache_v, lam, g_sub)


def _proj_c_kernel(x_ref, sh_ref, sc_ref, g_ref, w1_ref, gq_ref, wuq_ref, gkv_ref, wukv_ref,
                   cos_ref, slo_ref, shi_ref, q_ref, k_ref, v_ref, kvrow_ref, krrow_ref, *, transposed, tm):
    h = (_rms(x_ref[...], g_ref[...]) * (1.0 + sc_ref[0]) + sh_ref[0]).astype(BF16)
    cos, slo, shi = cos_ref[...], slo_ref[...], shi_ref[...]
    d = _dot(h, w1_ref[...])
    qlat = _rms(d[:, 0:Q_LORA], gq_ref[...]).astype(BF16)
    kv_row = _rms(d[:, Q_LORA:Q_LORA + KV_LORA], gkv_ref[...])
    kr = _rope_cols(d[:, Q_LORA + KV_LORA:], cos, slo, shi, ROPE_C // 2)
    kvrow_ref[...] = kv_row
    krrow_ref[...] = kr[:, 0:ROPE_C]
    kr_bf = kr.astype(BF16)
    q = _dot(qlat, wuq_ref[...]) * QSCALE_C
    kv = _dot(kv_row.astype(BF16), wukv_ref[...])

    def put(ref, cs, val):
        if transposed:
            vt = val.T.astype(BF16)
            for t in range(tm // TB):
                ref[0, t, cs, :] = vt[:, t * TB:(t + 1) * TB]
        else:
            ref[:, cs] = val.astype(BF16)

    for hh in range(H_C):
        lo = slice(hh * DQK_C_PAD, hh * DQK_C_PAD + LANES)
        hi = slice(hh * DQK_C_PAD + LANES, (hh + 1) * DQK_C_PAD)
        put(q_ref, lo, q[:, lo])
        put(q_ref, hi, _rope_cols(q[:, hi], cos, slo, shi, ROPE_C // 2))
        k_ref[:, lo] = kv[:, lo].astype(BF16)
        k_ref[:, hi] = kr_bf
        put(v_ref, slice(hh * LANES, (hh + 1) * LANES), kv[:, hi])


def _proj_c(path, x, sh, sc, g, w1, g_q, w_uq, g_kv, w_ukv, tables):
    n, tm, tpb = path.n, path.tm, path.tpb
    w1n = w1.shape[1]
    transposed = path.seq >= TM
    if transposed:
        tiled = lambda width: (jax.ShapeDtypeStruct((path.batch, path.seq // TB, width, TB), BF16),
                               pl.BlockSpec((1, tm // TB, width, TB), lambda i: (i // tpb, i % tpb, 0, 0)))
    else:
        tiled = lambda width: (jax.ShapeDtypeStruct((n, width), BF16), path.rows(width))
    q_shape, q_spec = tiled(H_C * DQK_C_PAD)
    v_shape, v_spec = tiled(H_C * V_C)
    return pl.pallas_call(
        functools.partial(_proj_c_kernel, transposed=transposed, tm=tm),
        grid=(path.steps,),
        in_specs=[path.rows(D_MODEL), path.mod(), path.mod(), _const((1, D_MODEL)),
                  _const((D_MODEL, w1n)), _const((1, Q_LORA)), _const((Q_LORA, H_C * DQK_C_PAD)),
                  _const((1, KV_LORA)), _const((KV_LORA, H_C * (NOPE_C + V_C))),
                  path.table(), path.table(), path.table()],
        out_specs=[q_spec, path.rows(H_C * DQK_C_PAD), v_spec, path.rows(KV_LORA), path.rows(ROPE_C)],
        out_shape=[q_shape, jax.ShapeDtypeStruct((n, H_C * DQK_C_PAD), BF16), v_shape,
                   jax.ShapeDtypeStruct((n, KV_LORA), F32), jax.ShapeDtypeStruct((n, ROPE_C), F32)],
        compiler_params=_params(("arbitrary",)),
        name="proj_c",
    )(x, sh, sc, g, w1, g_q, w_uq, g_kv, w_ukv, *tables)


def _decode_c_kernel(q_ref, kn_ref, vn_ref, kvp_ref, krp_ref, wukv_ref, o_ref):
    q = q_ref[0]
    kvh = _dot(kvp_ref[0, 0].astype(BF16), wukv_ref[...])
    k_past = kvh[:, 0:NOPE_C].astype(BF16)
    v_past = kvh[:, NOPE_C:].astype(BF16)
    kr_past = krp_ref[0, 0].astype(BF16)
    s_past = _dot_nt(q[:, 0:NOPE_C], k_past) + _dot_nt(q[:, NOPE_C:NOPE_C + ROPE_C], kr_past)
    p_past, p_new, l = _softmax_two(s_past, _dot_nt(q, kn_ref[0]))
    o = (_dot(p_past.astype(BF16), v_past) + _dot(p_new.astype(BF16), vn_ref[0])) / l
    o_ref[0] = o.astype(BF16)


def _decode_c(q, k, v, cache_kv, cache_kr, w_ukv, jl, batch, seq):
    past = cache_kv.shape[2]
    return pl.pallas_call(
        _decode_c_kernel,
        grid=(batch, H_C),
        in_specs=[pl.BlockSpec((1, seq, DQK_C_PAD), lambda b, h: (b, 0, h)),
                  pl.BlockSpec((1, seq, DQK_C_PAD), lambda b, h: (b, 0, h)),
                  pl.BlockSpec((1, seq, LANES), lambda b, h: (b, 0, h)),
                  pl.BlockSpec((1, 1, past, KV_LORA), lambda b, h: (jl, b, 0, 0)),
                  pl.BlockSpec((1, 1, past, ROPE_C), lambda b, h: (jl, b, 0, 0)),
                  pl.BlockSpec((KV_LORA, NOPE_C + V_C), lambda b, h: (0, h))],
        out_specs=pl.BlockSpec((1, seq, LANES), lambda b, h: (b, 0, h)),
        out_shape=jax.ShapeDtypeStruct((batch, seq, H_C * V_C), BF16),
        compiler_params=_params(("arbitrary", "arbitrary")),
        name="decode_c",
    )(q, k, v, cache_kv, cache_kr, w_ukv)


def _proj_b_kernel(x_ref, sh_ref, sc_ref, g_ref, w_ref, bg_ref,
                   q_ref, k_ref, kt_ref, v_ref, opre_ref, gates_ref, gt_ref):
    nqk, nv = H_B * DQK_B, H_B * DV_B
    h = (_rms(x_ref[...], g_ref[...]) * (1.0 + sc_ref[0]) + sh_ref[0]).astype(BF16)
    q_ref[...] = _dot(h, w_ref[:, 0:nqk]).astype(BF16)
    k = _dot(h, w_ref[:, nqk:2 * nqk]) * (DQK_B ** -0.5)
    k_ref[...] = k
    kt_ref[0] = k.T
    v_ref[...] = _dot(h, w_ref[:, 2 * nqk:2 * nqk + nv]).astype(BF16)
    opre_ref[...] = _dot(h, w_ref[:, 2 * nqk + nv:2 * nqk + 2 * nv])
    gates = _dot(h, w_ref[:, 2 * nqk + 2 * nv:]) + bg_ref[...]
    gates_ref[...] = gates
    gt_ref[0] = gates.T[0:2 * H_B, :]


def _proj_b(path, x, sh, sc, g, w_in, b_gates):
    n, nqk, nv = path.n, H_B * DQK_B, H_B * DV_B
    groups = path.n // path.table_rows
    tpb = path.tpb
    cols = lambda width: pl.BlockSpec((1, width, path.tm), lambda i: (i // tpb, 0, i % tpb))
    return pl.pallas_call(
        _proj_b_kernel,
        grid=(path.steps,),
        in_specs=[path.rows(D_MODEL), path.mod(), path.mod(), _const((1, D_MODEL)),
                  _const((D_MODEL, w_in.shape[1])), _const((1, LANES))],
        out_specs=[path.rows(nqk), path.rows(nqk), cols(nqk), path.rows(nv), path.rows(nv),
                   path.rows(LANES), cols(2 * H_B)],
        out_shape=[jax.ShapeDtypeStruct((n, nqk), BF16), jax.ShapeDtypeStruct((n, nqk), F32),
                   jax.ShapeDtypeStruct((groups, nqk, path.table_rows), F32),
                   jax.ShapeDtypeStruct((n, nv), BF16), jax.ShapeDtypeStruct((n, nv), F32),
                   jax.ShapeDtypeStruct((n, LANES), F32),
                   jax.ShapeDtypeStruct((groups, 2 * H_B, path.table_rows), F32)],
        compiler_params=_params(("arbitrary",)),
        name="proj_b",
    )(x, sh, sc, g, w_in, b_gates)


def _mlstm_kernel(q_ref, k_ref, kt_ref, v_ref, g_ref, gt_ref, c0_ref, n0_ref, m0_ref,
                  hs_ref, ct_ref, nt_ref, mt_ref, c_scr, n_scr, m_scr, *, chunk, chunks_per_step):
    step = pl.program_id(1)

    @pl.when(step == 0)
    def _():
        c_scr[...] = c0_ref[0]
        n_scr[...] = n0_ref[0]
        m_scr[...] = m0_ref[0]

    L, G = chunk, chunks_per_step
    row = lax.broadcasted_iota(jnp.int32, (L, L), 0)
    col = lax.broadcasted_iota(jnp.int32, (L, L), 1)
    tril = col <= row
    triu = row <= col
    units = [(c, h) for c in range(G) for h in range(H_B)]
    rows_of = lambda c: slice(c * L, (c + 1) * L)
    qk_cols = lambda h: slice(h * DQK_B, (h + 1) * DQK_B)
    v_cols = lambda h: slice(h * DV_B, (h + 1) * DV_B)
    stack = lambda parts: jnp.stack(parts, axis=0)

    g_col = g_ref[0]
    g_row = gt_ref[0]
    lf_col = jax.nn.log_sigmoid(g_col)
    lf_row = jax.nn.log_sigmoid(g_row)
    ig_col = stack([g_col[rows_of(c), h:h + 1] for c, h in units])
    f_col = stack([lf_col[rows_of(c), H_B + h:H_B + h + 1] for c, h in units])
    ig_row = stack([g_row[h:h + 1, rows_of(c)] for c, h in units])
    f_row = stack([lf_row[H_B + h:H_B + h + 1, rows_of(c)] for c, h in units])
    b_col = jnp.sum(jnp.where(tril, f_row, 0.0), axis=2, keepdims=True)
    b_row = jnp.sum(jnp.where(triu, f_col, 0.0), axis=1, keepdims=True)
    g_tot = jnp.sum(f_row, axis=2, keepdims=True)
    dm = jnp.where(tril, b_col - b_row + ig_row, NEG)
    dm_max = jnp.max(dm, axis=2, keepdims=True)
    r_row = g_tot - b_row + ig_row
    r_col = g_tot - b_col + ig_col
    r_max = jnp.max(r_row, axis=2, keepdims=True)

    m_steps = [m_scr[...]]
    for c in range(G):
        heads = slice(c * H_B, (c + 1) * H_B)
        m_steps.append(jnp.maximum(g_tot[heads] + m_steps[c], r_max[heads]))
    m_cur = jnp.concatenate(m_steps[:G], axis=0)
    m_next = jnp.concatenate(m_steps[1:], axis=0)
    inter = b_col + m_cur
    m_t = jnp.maximum(inter, dm_max)
    a = jnp.exp(inter - m_t)
    floor = jnp.exp(-m_t)
    w_in = jnp.exp(dm - m_t)
    wr_row = jnp.exp(r_row - m_next)
    wr_col = jnp.exp(r_col - m_next)
    decay = jnp.exp(g_tot + m_cur - m_next)

    qk = stack([_dot_nt(q_ref[0, rows_of(c), qk_cols(h)], k_ref[0, rows_of(c), qk_cols(h)].astype(BF16))
                for c, h in units]) * w_in
    qk_sum = jnp.sum(qk, axis=2, keepdims=True)
    c_add, n_add, within = [], [], []
    for u, (c, h) in enumerate(units):
        vh = v_ref[0, rows_of(c), v_cols(h)]
        kw_t = kt_ref[0, qk_cols(h), rows_of(c)] * wr_row[u]
        c_add.append(_dot(kw_t.astype(BF16), vh))
        n_add.append(jnp.sum(k_ref[0, rows_of(c), qk_cols(h)] * wr_col[u], axis=0, keepdims=True))
        within.append(_dot(qk[u].astype(BF16), vh))

    c_cur = [c_scr[h] for h in range(H_B)]
    n_cur = [n_scr[h] for h in range(H_B)]
    for c in range(G):
        heads = slice(c * H_B, (c + 1) * H_B)
        q_heads = [q_ref[0, rows_of(c), qk_cols(h)] for h in range(H_B)]
        from_memory = [_dot(q_heads[h], c_cur[h].astype(BF16)) for h in range(H_B)]
        qn = jnp.sum(stack(q_heads).astype(F32) * stack(n_cur), axis=2, keepdims=True)
        den = a[heads] * qn + qk_sum[heads]
        div = jnp.maximum(jnp.abs(den), floor[heads])
        for h in range(H_B):
            u = c * H_B + h
            hs_ref[0, rows_of(c), v_cols(h)] = (a[u] * from_memory[h] + within[u]) / div[h]
            c_cur[h] = decay[u] * c_cur[h] + c_add[u]
            n_cur[h] = decay[u] * n_cur[h] + n_add[u]
    for h in range(H_B):
        c_scr[h] = c_cur[h]
        n_scr[h] = n_cur[h]
    m_scr[...] = m_steps[G]

    @pl.when(step == pl.num_programs(1) - 1)
    def _():
        ct_ref[0] = c_scr[...]
        nt_ref[0] = n_scr[...]
        mt_ref[0] = m_scr[...]


def _mlstm(q, k, kt, v, gates, gt, c0, n0, m0, batch, seq):
    chunk = CHUNK if seq % CHUNK == 0 else seq
    cps = max(1, min(MLSTM_SPAN // chunk, seq // chunk))
    span = chunk * cps
    nqk, nv = H_B * DQK_B, H_B * DV_B
    rows = lambda width: pl.BlockSpec((1, span, width), lambda b, s: (b, s, 0))
    cols = lambda height: pl.BlockSpec((1, height, span), lambda b, s: (b, 0, s))
    state = lambda *shape: pl.BlockSpec((1,) + shape, lambda b, s: (b,) + (0,) * len(shape))
    return pl.pallas_call(
        functools.partial(_mlstm_kernel, chunk=chunk, chunks_per_step=cps),
        grid=(batch, seq // span),
        in_specs=[rows(nqk), rows(nqk), cols(nqk), rows(nv), rows(LANES), cols(2 * H_B),
                  state(H_B, DQK_B, DV_B), state(H_B, 1, DQK_B), state(H_B, 1, 1)],
        out_specs=[rows(nv), state(H_B, DQK_B, DV_B), state(H_B, 1, DQK_B), state(H_B, 1, 1)],
        out_shape=[jax.ShapeDtypeStruct((batch, seq, nv), F32),
                   jax.ShapeDtypeStruct((batch, H_B, DQK_B, DV_B), F32),
                   jax.ShapeDtypeStruct((batch, H_B, 1, DQK_B), F32),
                   jax.ShapeDtypeStruct((batch, H_B, 1, 1), F32)],
        scratch_shapes=[pltpu.VMEM((H_B, DQK_B, DV_B), F32), pltpu.VMEM((H_B, 1, DQK_B), F32),
                        pltpu.VMEM((H_B, 1, 1), F32)],
        compiler_params=_params(("arbitrary", "arbitrary")),
        name="mlstm",
    )(q, k, kt, v, gates, gt, c0, n0, m0)


def _tail_kernel(*refs, gated_heads, final):
    if gated_heads:
        hs_ref, opre_ref, gout_ref = refs[:3]
        refs = refs[3:]
        hs, gout = hs_ref[...], gout_ref[...]
        normed = [_rms(hs[:, h * DV_B:(h + 1) * DV_B], gout[:, h * DV_B:(h + 1) * DV_B]) for h in range(H_B)]
        mixed = (jax.nn.sigmoid(opre_ref[...]) * jnp.concatenate(normed, axis=-1)).astype(BF16)
    else:
        mixed = refs[0][...]
        refs = refs[1:]
    (x_ref, gt1_ref, wo_ref, sh_ref, sc_ref, gt_ref, g_ref, win_ref, wout_ref, gfin_ref, y_ref, acc_scr) = refs
    x = x_ref[...] + gt1_ref[0] * _dot(mixed, wo_ref[...])
    h = (_rms(x, g_ref[...]) * (1.0 + sc_ref[0]) + sh_ref[0]).astype(BF16)
    for c in range(D_FF // FF_CHUNK):
        a = _dot(h, win_ref[:, c * FF_CHUNK:(c + 1) * FF_CHUNK])
        b = _dot(h, win_ref[:, D_FF + c * FF_CHUNK:D_FF + (c + 1) * FF_CHUNK])
        act = (a * jax.nn.sigmoid(a) * b).astype(BF16)
        part = _dot(act, wout_ref[c * FF_CHUNK:(c + 1) * FF_CHUNK, :])
        if c == 0:
            acc_scr[...] = part
        else:
            acc_scr[...] += part
    y = x + gt_ref[0] * acc_scr[...]
    y_ref[...] = _rms(y, gfin_ref[...]) if final else y


def _tail(path, mixer_out, x, gt1, w_o, sh, sc, gt, g, w_in, w_out, g_final, final):
    gated_heads = len(mixer_out) == 3
    mixer_specs = [path.rows(D_MODEL)]
    if gated_heads:
        mixer_specs += [path.rows(D_MODEL), _const((1, D_MODEL))]
    return pl.pallas_call(
        functools.partial(_tail_kernel, gated_heads=gated_heads, final=final),
        grid=(path.steps,),
        in_specs=mixer_specs + [path.rows(D_MODEL), path.mod(), _const((D_MODEL, D_MODEL)),
                                path.mod(), path.mod(), path.mod(), _const((1, D_MODEL)),
                                _const((D_MODEL, 2 * D_FF)), _const((D_FF, D_MODEL)), _const((1, D_MODEL))],
        out_specs=path.rows(D_MODEL),
        out_shape=jax.ShapeDtypeStruct((path.n, D_MODEL), F32),
        scratch_shapes=[pltpu.VMEM((path.tm, D_MODEL), F32)],
        compiler_params=_params(("arbitrary",)),
        name="tail_b" if gated_heads else "tail",
    )(*mixer_out, x, gt1, w_o, sh, sc, gt, g, w_in, w_out, g_final)


def _prep_weights(p):
    w = {}
    w['a_qkv'] = p['w_a_qkv'].astype(BF16)
    w['a_o'] = p['w_a_o'].astype(BF16)
    nqk, nv = H_B * DQK_B, H_B * DV_B
    w_b = p['w_b_in']
    gate_pad = jnp.zeros(w_b.shape[:2] + (LANES - 2 * H_B,), w_b.dtype)
    w['b_in'] = jnp.concatenate([w_b, gate_pad], axis=-1).astype(BF16)
    bg = p['b_b_gates']
    w['b_gates'] = jnp.concatenate([bg, jnp.zeros((bg.shape[0], LANES - 2 * H_B), bg.dtype)], axis=-1)[:, None, :]
    w['b_out'] = p['w_b_out'].astype(BF16)
    n_c = p['w_c_dq'].shape[0]
    w_dkv = p['w_c_dkv']
    w['c_1'] = jnp.concatenate([p['w_c_dq'], w_dkv, jnp.zeros((n_c, D_MODEL, LANES - ROPE_C), w_dkv.dtype)],
                               axis=-1).astype(BF16)
    w_uq = p['w_c_uq'].reshape(n_c, Q_LORA, H_C, NOPE_C + ROPE_C)
    w_uq = jnp.concatenate([w_uq, jnp.zeros((n_c, Q_LORA, H_C, DQK_C_PAD - NOPE_C - ROPE_C), w_uq.dtype)], axis=-1)
    w['c_uq'] = w_uq.reshape(n_c, Q_LORA, H_C * DQK_C_PAD).astype(BF16)
    w['c_ukv'] = p['w_c_ukv'].astype(BF16)
    w['c_o'] = p['w_c_o'].astype(BF16)
    w['ffn_in'] = p['w_ffn_in'].astype(BF16)
    w['ffn_out'] = p['w_ffn_out'].astype(BF16)
    return w


def _trunk(x, mods, past, p, w):
    batch, seq, _ = x.shape
    path = _Path(batch, seq)
    n_past = 0 if past is None else past[5].shape[2]
    pos = n_past + jnp.arange(seq, dtype=jnp.int32)
    tab_a = tuple(path.expand_table(t) for t in _rope_tables(pos, ROT_A, D_SUB))
    tab_c = tuple(path.expand_table(t) for t in _rope_tables(pos, ROPE_C, LANES))
    x = x.reshape(path.n, D_MODEL)
    g_final = p['g_final'][None, :]
    n_a = (DEPTH + N_MIXERS - 1) // N_MIXERS
    a_rows = None
    b_c, b_n, b_m, c_kv, c_kr = [], [], [], [], []
    for i in range(DEPTH):
        kind, j = i % N_MIXERS, i // N_MIXERS
        sh1, sc1, gt1, sh2, sc2, gt2 = (path.expand_mod(m) for m in jnp.split(mods[i], 6, axis=-1))
        g1 = p['g_norm1'][i][None, :]
        if kind == 0:
            lambda_init = 0.8 - 0.6 * math.exp(-0.3 * i)
            q, k, v, k_row, v_row = _proj_a(path, x, sh1, sc1, g1, w['a_qkv'][j], tab_a, j, n_a, a_rows)
            a_rows = (k_row, v_row)
            shape3 = (batch, seq, D_MODEL)
            if past is None:
                g_sub_t = jnp.broadcast_to(p['g_a_sub'][:, :, None], p['g_a_sub'].shape + (TB,))
                o = _flash(q, k.reshape(shape3), v, batch, seq, H_A, LANES,
                           diff_args=(p['a_lambda'], g_sub_t, j), lambda_init=lambda_init)
            else:
                o = _decode_a(q.reshape(shape3), k.reshape(shape3), v.reshape(shape3), past[0], past[1],
                              p['a_lambda'], p['g_a_sub'][:, None, :], j, lambda_init, batch, seq)
            mixer_out, w_o = (o.reshape(path.n, D_MODEL),), w['a_o'][j]
        elif kind == 1:
            q, k, kt, v, opre, gates, gt = _proj_b(path, x, sh1, sc1, g1, w['b_in'][j], w['b_gates'][j])
            nqk, nv = H_B * DQK_B, H_B * DV_B
            if path.table_rows != seq:
                kt = kt.reshape(nqk, batch, seq).transpose(1, 0, 2)
                gt = gt.reshape(2 * H_B, batch, seq).transpose(1, 0, 2)
            if past is None:
                c0 = jnp.zeros((batch, H_B, DQK_B, DV_B), F32)
                n0 = jnp.zeros((batch, H_B, 1, DQK_B), F32)
                m0 = jnp.zeros((batch, H_B, 1, 1), F32)
            else:
                c0 = past[2][j]
                n0 = past[3][j][:, :, None, :]
                m0 = past[4][j][:, :, None, None]
            hs, c_t, n_t, m_t = _mlstm(q.reshape(batch, seq, nqk), k.reshape(batch, seq, nqk), kt,
                                       v.reshape(batch, seq, nv), gates.reshape(batch, seq, LANES), gt,
                                       c0, n0, m0, batch, seq)
            b_c.append(c_t)
            b_n.append(n_t[:, :, 0, :])
            b_m.append(m_t[:, :, 0, 0])
            mixer_out, w_o = (hs.reshape(path.n, nv), opre, p['g_b_out'][j][None, :]), w['b_out'][j]
        else:
            q, k, v, kv_row, kr_row = _proj_c(path, x, sh1, sc1, g1, w['c_1'][j], p['g_c_q'][j][None, :],
                                              w['c_uq'][j], p['g_c_kv'][j][None, :], w['c_ukv'][j], tab_c)
            qk_shape = (batch, seq, H_C * DQK_C_PAD)
            v_shape = (batch, seq, H_C * V_C)
            if past is None:
                o = _flash(q, k.reshape(qk_shape), v, batch, seq, H_C, DQK_C_PAD)
            else:
                o = _decode_c(q.reshape(qk_shape), k.reshape(qk_shape), v.reshape(v_shape), past[5], past[6],
                              w['c_ukv'][j], j, batch, seq)
            c_kv.append(kv_row.reshape(batch, seq, KV_LORA))
            c_kr.append(kr_row.reshape(batch, seq, ROPE_C))
            mixer_out, w_o = (o.reshape(path.n, D_MODEL),), w['c_o'][j]
        x = _tail(path, mixer_out, x, gt1, w_o, sh2, sc2, gt2, p['g_norm2'][i][None, :],
                  w['ffn_in'][i], w['ffn_out'][i], g_final, final=(i == DEPTH - 1))
    y = x.reshape(batch, seq, D_MODEL)
    a_k, a_v = (r.reshape(n_a, batch, seq, H_A, 2 * D_SUB) for r in a_rows)
    return y, (a_k, a_v, jnp.stack(b_c), jnp.stack(b_n), jnp.stack(b_m), jnp.stack(c_kv), jnp.stack(c_kr))


def kernel(x_prompt, x_sample, c_prompt, c_sample, cache_a_k, cache_a_v, state_b_c, state_b_n, state_b_m, cache_c_kv, cache_c_kr, w_ada, b_ada, g_norm1, g_norm2, w_a_qkv, a_lambda, g_a_sub, w_a_o, w_b_in, b_b_gates, g_b_out, w_b_out, w_c_dq, g_c_q, w_c_uq, w_c_dkv, g_c_kv, w_c_ukv, w_c_o, w_ffn_in, w_ffn_out, g_final):
    p = dict(g_norm1=g_norm1, g_norm2=g_norm2, w_a_qkv=w_a_qkv, a_lambda=a_lambda, g_a_sub=g_a_sub,
             w_a_o=w_a_o, w_b_in=w_b_in, b_b_gates=b_b_gates, g_b_out=g_b_out, w_b_out=w_b_out,
             w_c_dq=w_c_dq, g_c_q=g_c_q, w_c_uq=w_c_uq, w_c_dkv=w_c_dkv, g_c_kv=g_c_kv, w_c_ukv=w_c_ukv,
             w_c_o=w_c_o, w_ffn_in=w_ffn_in, w_ffn_out=w_ffn_out, g_final=g_final)
    w = _prep_weights(p)
    nb_p = x_prompt.shape[0]
    mods = _ada(jnp.concatenate([c_prompt, c_sample], axis=0), w_ada, b_ada)
    y_prompt, sp = _trunk(x_prompt, mods[:, :nb_p], None, p, w)
    past = (cache_a_k.reshape(-1, 2 * D_SUB), cache_a_v.reshape(-1, 2 * D_SUB),
            state_b_c, state_b_n, state_b_m, cache_c_kv, cache_c_kr)
    y_sample, ss = _trunk(x_sample, mods[:, nb_p:], past, p, w)
    return (y_prompt, y_sample) + sp + ss
```

```python
import functools
import math

import jax
import jax.numpy as jnp
from jax import lax
from jax.experimental import pallas as pl
from jax.experimental.pallas import tpu as pltpu

F32 = jnp.float32
BF16 = jnp.bfloat16

D_MODEL = 1024
DEPTH = 4
CHUNK = 64
N_MIXERS = 3
ROPE_THETA = 500000.0
EPS = 1e-6
D_SUB = 64
H_A = D_MODEL // (2 * D_SUB)
ROT_A = D_SUB // 4
H_B = 4
DQK_B = D_MODEL // (2 * H_B)
DV_B = D_MODEL // H_B
H_C = 8
NOPE_C = 128
ROPE_C = 64
V_C = 128
Q_LORA = 384
KV_LORA = 256
D_FF = -(-8 * D_MODEL // (3 * 256)) * 256

LANES = 128
BF16_ROWS = 16
DQK_C_PAD = 2 * LANES
NEG = -1e30
VMEM_LIMIT = 56 * 1024 * 1024
TM = 512
TB = 256
HEAD_GROUP = 4
FF_CHUNK = 256
MLSTM_SPAN = 256
LOG2E = math.log2(math.e)
QSCALE_A = D_SUB ** -0.5 * LOG2E
QSCALE_C = (NOPE_C + ROPE_C) ** -0.5 * LOG2E


def _params(sem):
    return pltpu.CompilerParams(dimension_semantics=sem, vmem_limit_bytes=VMEM_LIMIT)


def _const(shape):
    nd = len(shape)
    return pl.BlockSpec(shape, lambda *_: (0,) * nd, pipeline_mode=pl.Buffered(1))


def _rms(x, g):
    return x * lax.rsqrt(jnp.mean(x * x, axis=-1, keepdims=True) + EPS) * g


def _dot(a, b):
    return jnp.dot(a, b, preferred_element_type=F32)


def _dot_nt(a, b):
    return lax.dot_general(a, b, (((1,), (1,)), ((), ())), preferred_element_type=F32)


def _rope_cols(c, cos, sin_lo, sin_hi, half):
    return c * cos + pltpu.roll(c, LANES - half, 1) * sin_lo + pltpu.roll(c, half, 1) * sin_hi


def _ada_kernel(c_ref, w_ref, b_ref, o_ref):
    c = c_ref[...]
    s = (c * jax.nn.sigmoid(c)).astype(BF16)
    o_ref[0] = _dot(s, w_ref[0].astype(BF16)) + b_ref[0]


def _ada(c_all, w_ada, b_ada):
    n = c_all.shape[0]
    tn = 1536
    return pl.pallas_call(
        _ada_kernel,
        grid=(DEPTH, 6 * D_MODEL // tn),
        in_specs=[pl.BlockSpec((n, D_MODEL), lambda i, j: (0, 0)),
                  pl.BlockSpec((1, D_MODEL, tn), lambda i, j: (i, 0, j)),
                  pl.BlockSpec((1, 1, tn), lambda i, j: (i, 0, j))],
        out_specs=pl.BlockSpec((1, n, tn), lambda i, j: (i, 0, j)),
        out_shape=jax.ShapeDtypeStruct((DEPTH, n, 6 * D_MODEL), F32),
        compiler_params=_params(("arbitrary", "arbitrary")),
        name="ada",
    )(c_all, w_ada, b_ada.reshape(DEPTH, 1, 6 * D_MODEL))


class _Path:
    def __init__(self, batch, seq):
        self.batch, self.seq = batch, seq
        self.n = batch * seq
        if seq >= TM:
            self.tm, self.tpb, self.mod_rows = TM, seq // TM, 1
        else:
            self.tm, self.tpb, self.mod_rows = self.n, 1, self.n
        self.steps = self.n // self.tm
        self.table_rows = self.tpb * self.tm

    def rows(self, width):
        return pl.BlockSpec((self.tm, width), lambda i: (i, 0))

    def mod(self):
        tpb = self.tpb
        return pl.BlockSpec((1, self.mod_rows, D_MODEL), lambda i: (i // tpb, 0, 0))

    def table(self):
        tpb = self.tpb
        return pl.BlockSpec((self.tm, LANES), lambda i: (i % tpb, 0))

    def expand_mod(self, m):
        if self.mod_rows == 1:
            return m[:, None, :]
        return jnp.repeat(m, self.seq, axis=0)[None]

    def expand_table(self, t):
        if self.table_rows == self.seq:
            return t
        return jnp.tile(t, (self.table_rows // self.seq, 1))


def _rope_tables(pos, rot, period):
    half = rot // 2
    inv = jnp.power(jnp.float32(ROPE_THETA), -jnp.arange(half, dtype=jnp.float32) * (2.0 / rot))
    ang = pos.astype(jnp.float32)[:, None] * inv[None, :]
    cos, sin = jnp.cos(ang), jnp.sin(ang)
    n = pos.shape[0]
    reps = LANES // period
    pad1 = jnp.ones((n, period - rot), F32)
    pad0 = jnp.zeros((n, period - rot), F32)
    zh = jnp.zeros((n, half), F32)
    cos_t = jnp.tile(jnp.concatenate([cos, cos, pad1], axis=1), (1, reps))
    sin_lo = jnp.tile(jnp.concatenate([-sin, zh, pad0], axis=1), (1, reps))
    sin_hi = jnp.tile(jnp.concatenate([zh, sin, pad0], axis=1), (1, reps))
    return cos_t, sin_lo, sin_hi


def _proj_a_kernel(*refs, transposed, aliased, tm, slot):
    refs = refs[2:] if aliased else refs
    (x_ref, sh_ref, sc_ref, g_ref, w_ref, cos_ref, slo_ref, shi_ref,
     q_ref, k_ref, v_ref, krow_all, vrow_all) = refs
    if aliased:
        krow_ref, vrow_ref = krow_all.at[0], vrow_all.at[0]
    else:
        krow_ref, vrow_ref = krow_all.at[slot], vrow_all.at[slot]
        for other in range(krow_all.shape[0]):
            if other != slot:
                krow_all[other] = jnp.zeros(krow_all.shape[1:], F32)
                vrow_all[other] = jnp.zeros(vrow_all.shape[1:], F32)
    h = (_rms(x_ref[...], g_ref[...]) * (1.0 + sc_ref[0]) + sh_ref[0]).astype(BF16)
    cos, slo, shi = cos_ref[...], slo_ref[...], shi_ref[...]
    q = _dot(h, w_ref[:, 0:D_MODEL])
    k = _dot(h, w_ref[:, D_MODEL:2 * D_MODEL])
    v = _dot(h, w_ref[:, 2 * D_MODEL:3 * D_MODEL])
    for j in range(H_A):
        cs = slice(j * LANES, (j + 1) * LANES)
        qj = _rope_cols(q[:, cs], cos, slo, shi, ROT_A // 2) * QSCALE_A
        kj = _rope_cols(k[:, cs], cos, slo, shi, ROT_A // 2)
        vj = v[:, cs]
        krow_ref[pl.ds(j, tm, stride=H_A), :] = kj
        vrow_ref[pl.ds(j, tm, stride=H_A), :] = vj
        k_ref[:, cs] = kj.astype(BF16)
        if transposed:
            qt, vt = qj.T.astype(BF16), vj.T.astype(BF16)
            for t in range(tm // TB):
                q_ref[0, t, cs, :] = qt[:, t * TB:(t + 1) * TB]
                v_ref[0, t, cs, :] = vt[:, t * TB:(t + 1) * TB]
        else:
            q_ref[:, cs] = qj.astype(BF16)
            v_ref[:, cs] = vj.astype(BF16)


def _proj_a(path, x, sh, sc, g, w_qkv, tables, layer_slot, n_slots, prev_rows):
    n, tm, tpb, steps = path.n, path.tm, path.tpb, path.steps
    transposed = path.seq >= TM
    aliased = prev_rows is not None
    bf = jax.ShapeDtypeStruct((n, D_MODEL), BF16)
    if transposed:
        bft = jax.ShapeDtypeStruct((path.batch, path.seq // TB, D_MODEL, TB), BF16)
        tspec = pl.BlockSpec((1, tm // TB, D_MODEL, TB), lambda i: (i // tpb, i % tpb, 0, 0))
        qv_shape, qv_spec = bft, tspec
    else:
        qv_shape, qv_spec = bf, path.rows(D_MODEL)
    rows_shape = jax.ShapeDtypeStruct((n_slots, n * H_A, LANES), F32)
    if aliased:
        rows_spec = pl.BlockSpec((1, tm * H_A, LANES), lambda i: (layer_slot, i, 0))
    else:
        rows_spec = pl.BlockSpec((n_slots, tm * H_A, LANES), lambda i: (0, i, 0))
    in_specs = [path.rows(D_MODEL), path.mod(), path.mod(), _const((1, D_MODEL)),
                _const((D_MODEL, 3 * D_MODEL)), path.table(), path.table(), path.table()]
    args = [x, sh, sc, g, w_qkv, *tables]
    aliases = {}
    if aliased:
        in_specs = [pl.BlockSpec(memory_space=pl.ANY)] * 2 + in_specs
        args = list(prev_rows) + args
        aliases = {0: 3, 1: 4}
    return pl.pallas_call(
        functools.partial(_proj_a_kernel, transposed=transposed, aliased=aliased, tm=tm, slot=layer_slot),
        grid=(steps,),
        in_specs=in_specs,
        out_specs=[qv_spec, path.rows(D_MODEL), qv_spec, rows_spec, rows_spec],
        out_shape=[qv_shape, bf, qv_shape, rows_shape, rows_shape],
        input_output_aliases=aliases,
        compiler_params=_params(("arbitrary",)),
        name="proj_a",
    )(*args)


def _lambda_full(lam_ref, lambda_init):
    lv = lam_ref[0]
    d1 = jnp.sum(lv[0:1] * lv[1:2], axis=-1, keepdims=True)
    d2 = jnp.sum(lv[2:3] * lv[3:4], axis=-1, keepdims=True)
    return jnp.exp(d1) - jnp.exp(d2) + lambda_init


def _split_subheads(q):
    lane = lax.broadcasted_iota(jnp.int32, q.shape, 1)
    zero = jnp.zeros_like(q)
    return [jnp.where(lane < D_SUB, q, zero), jnp.where(lane >= D_SUB, q, zero)]


def _diff_combine(o1, o2, lam, g_sub, lambda_init):
    o = o1 - lam * o2
    return _rms(o, g_sub) * (1.0 - lambda_init)


def _flash_kernel(*refs, diff, lambda_init, n_tiles, heads, dqk):
    if diff:
        qt_ref, k_ref, vt_ref, lam_ref, gsub_ref, o_ref, m_scr, acc_scr, sa_scr, sb_scr = refs
    else:
        qt_ref, k_ref, vt_ref, o_ref, m_scr, acc_scr, sa_scr, sb_scr = refs
    n_sub = 2 if diff else 1
    key = lax.broadcasted_iota(jnp.int32, (TB, TB), 0)
    qry = lax.broadcasted_iota(jnp.int32, (TB, TB), 1)
    visible = (key // CHUNK) <= (qry // CHUNK)
    ones_rows = jnp.ones((BF16_ROWS, TB), BF16)

    def normalised(c):
        acc = acc_scr[c]
        return acc[0:LANES] / acc[LANES:LANES + 1]

    def chain_queries(i):
        qs = []
        for h in range(heads):
            qt = qt_ref[0, i, h * dqk:(h + 1) * dqk, :]
            if diff:
                dim = lax.broadcasted_iota(jnp.int32, qt.shape, 0)
                zero = jnp.zeros_like(qt)
                qs += [jnp.where(dim < D_SUB, qt, zero), jnp.where(dim >= D_SUB, qt, zero)]
            else:
                qs.append(qt)
        return qs

    def score_block(qs, j, s_scr):
        rows = pl.ds(pl.multiple_of(j * TB, TB), TB)
        for h in range(heads):
            kb = k_ref[0, rows, h * dqk:(h + 1) * dqk]
            for s in range(n_sub):
                s_scr[h * n_sub + s] = _dot(kb, qs[h * n_sub + s])

    score_block(chain_queries(0), 0, sa_scr)

    def q_tile(i, carry):
        qs = chain_queries(i)
        scores = functools.partial(score_block, qs)
        m_scr[...] = jnp.full(m_scr.shape, NEG, F32)
        acc_scr[...] = jnp.zeros(acc_scr.shape, F32)

        def update(j, s_scr, masked):
            for h in range(heads):
                vb = jnp.concatenate([vt_ref[0, j, h * LANES:(h + 1) * LANES, :], ones_rows], axis=0)
                for s in range(n_sub):
                    c = h * n_sub + s
                    st = jnp.where(visible, s_scr[c], NEG) if masked else s_scr[c]
                    m_prev = m_scr[c]
                    m_new = jnp.maximum(m_prev, jnp.max(st, axis=0, keepdims=True))
                    p = jnp.exp2(st - m_new)
                    acc_scr[c] = jnp.exp2(m_prev - m_new) * acc_scr[c] + _dot(vb, p.astype(BF16))
                    m_scr[c] = m_new

        def pair(t, c):
            scores(2 * t + 1, sb_scr)
            update(2 * t, sa_scr, False)
            scores(2 * t + 2, sa_scr)
            update(2 * t + 1, sb_scr, False)
            return c

        lax.fori_loop(0, i // 2, pair, 0)

        @pl.when(i % 2 == 0)
        def _():
            update(i, sa_scr, True)

        @pl.when(i % 2 == 1)
        def _():
            scores(i, sb_scr)
            update(i - 1, sa_scr, False)
            update(i, sb_scr, True)

        score_block(chain_queries(jnp.minimum(i + 1, n_tiles - 1)), 0, sa_scr)
        out_rows = pl.ds(pl.multiple_of(i * TB, TB), TB)
        for h in range(heads):
            if diff:
                ot = normalised(2 * h) - _lambda_full(lam_ref, lambda_init) * normalised(2 * h + 1)
                ot = (ot * lax.rsqrt(jnp.mean(ot * ot, axis=0, keepdims=True) + EPS) * gsub_ref[0]
                      * (1.0 - lambda_init))
            else:
                ot = normalised(h)
            o_ref[0, out_rows, h * LANES:(h + 1) * LANES] = ot.T.astype(BF16)
        return carry

    lax.fori_loop(0, n_tiles, q_tile, 0)


def _flash(qt, k, vt, batch, seq, heads, dqk, diff_args=None, lambda_init=0.0):
    diff = diff_args is not None
    hg = HEAD_GROUP
    n_tiles = seq // TB
    chains = hg * (2 if diff else 1)
    in_specs = [pl.BlockSpec((1, n_tiles, hg * dqk, TB), lambda b, g: (b, 0, g, 0)),
                pl.BlockSpec((1, seq, hg * dqk), lambda b, g: (b, 0, g)),
                pl.BlockSpec((1, n_tiles, hg * LANES, TB), lambda b, g: (b, 0, g, 0))]
    args = [qt, k, vt]
    if diff:
        lam, g_sub_t, jl = diff_args
        in_specs += [pl.BlockSpec((1, 4, D_SUB), lambda b, g: (jl, 0, 0)),
                     pl.BlockSpec((1, LANES, TB), lambda b, g: (jl, 0, 0))]
        args += [lam, g_sub_t]
    return pl.pallas_call(
        functools.partial(_flash_kernel, diff=diff, lambda_init=lambda_init, n_tiles=n_tiles, heads=hg, dqk=dqk),
        grid=(batch, heads // hg),
        in_specs=in_specs,
        out_specs=pl.BlockSpec((1, seq, hg * LANES), lambda b, g: (b, 0, g)),
        out_shape=jax.ShapeDtypeStruct((batch, seq, heads * LANES), BF16),
        scratch_shapes=[pltpu.VMEM((chains, 1, TB), F32),
                        pltpu.VMEM((chains, LANES + BF16_ROWS, TB), F32),
                        pltpu.VMEM((chains, TB, TB), F32), pltpu.VMEM((chains, TB, TB), F32)],
        compiler_params=_params(("arbitrary", "arbitrary")),
        name="flash_diff" if diff else "flash_mla",
    )(*args)


def _softmax_two(s_past, s_new):
    m = jnp.maximum(jnp.max(s_past, axis=-1, keepdims=True), jnp.max(s_new, axis=-1, keepdims=True))
    p_past = jnp.exp2(s_past - m)
    p_new = jnp.exp2(s_new - m)
    l = jnp.sum(p_past, axis=-1, keepdims=True) + jnp.sum(p_new, axis=-1, keepdims=True)
    return p_past, p_new, l


def _decode_a_kernel(q_ref, kn_ref, vn_ref, kp_ref, vp_ref, lam_ref, gsub_ref, o_ref, *, lambda_init, past, seq):
    lam = _lambda_full(lam_ref, lambda_init)
    q_all, kn_all, vn_all = q_ref[0], kn_ref[0], vn_ref[0]
    s_past, s_new = [], []
    for h in range(H_A):
        cs = slice(h * LANES, (h + 1) * LANES)
        q2 = jnp.concatenate(_split_subheads(q_all[:, cs]), axis=0)
        kp = kp_ref[pl.ds(h, past, stride=H_A), :].astype(BF16)
        s_past.append(_dot_nt(q2, kp))
        s_new.append(_dot_nt(q2, kn_all[:, cs]))
    for h in range(H_A):
        cs = slice(h * LANES, (h + 1) * LANES)
        vp = vp_ref[pl.ds(h, past, stride=H_A), :].astype(BF16)
        p_past, p_new, l = _softmax_two(s_past[h], s_new[h])
        o2 = (_dot(p_past.astype(BF16), vp) + _dot(p_new.astype(BF16), vn_all[:, cs])) / l
        o = _diff_combine(o2[0:seq], o2[seq:2 * seq], lam, gsub_ref[0], lambda_init)
        o_ref[0, :, cs] = o.astype(BF16)


def _decode_a(q, k, v, cache_k, cache_v, lam, g_sub, jl, lambda_init, batch, seq):
    past = cache_k.shape[0] // (H_A * batch * lam.shape[0])
    new = pl.BlockSpec((1, seq, D_MODEL), lambda b: (b, 0, 0))
    old = pl.BlockSpec((past * H_A, LANES), lambda b: (jl * batch + b, 0))
    return pl.pallas_call(
        functools.partial(_decode_a_kernel, lambda_init=lambda_init, past=past, seq=seq),
        grid=(batch,),
        in_specs=[new, new, new, old, old,
                  pl.BlockSpec((1, 4, D_SUB), lambda b: (jl, 0, 0)),
                  pl.BlockSpec((1, 1, LANES), lambda b: (jl, 0, 0))],
        out_specs=new,
        out_shape=jax.ShapeDtypeStruct((batch, seq, D_MODEL), BF16),
        compiler_params=_params(("arbitrary",)),
        name="decode_a",
    )(q, k, v, cache_k, cache_v, lam, g_sub)


def _proj_c_kernel(x_ref, sh_ref, sc_ref, g_ref, w1_ref, gq_ref, wuq_ref, gkv_ref, wukv_ref,
                   cos_ref, slo_ref, shi_ref, q_ref, k_ref, v_ref, kvrow_ref, krrow_ref, *, transposed, tm):
    h = (_rms(x_ref[...], g_ref[...]) * (1.0 + sc_ref[0]) + sh_ref[0]).astype(BF16)
    cos, slo, shi = cos_ref[...], slo_ref[...], shi_ref[...]
    d = _dot(h, w1_ref[...])
    qlat = _rms(d[:, 0:Q_LORA], gq_ref[...]).astype(BF16)
    kv_row = _rms(d[:, Q_LORA:Q_LORA + KV_LORA], gkv_ref[...])
    kr = _rope_cols(d[:, Q_LORA + KV_LORA:], cos, slo, shi, ROPE_C // 2)
    kvrow_ref[...] = kv_row
    krrow_ref[...] = kr[:, 0:ROPE_C]
    kr_bf = kr.astype(BF16)
    q = _dot(qlat, wuq_ref[...]) * QSCALE_C
    kv = _dot(kv_row.astype(BF16), wukv_ref[...])

    def put(ref, cs, val):
        if transposed:
            vt = val.T.astype(BF16)
            for t in range(tm // TB):
                ref[0, t, cs, :] = vt[:, t * TB:(t + 1) * TB]
        else:
            ref[:, cs] = val.astype(BF16)

    for hh in range(H_C):
        lo = slice(hh * DQK_C_PAD, hh * DQK_C_PAD + LANES)
        hi = slice(hh * DQK_C_PAD + LANES, (hh + 1) * DQK_C_PAD)
        put(q_ref, lo, q[:, lo])
        put(q_ref, hi, _rope_cols(q[:, hi], cos, slo, shi, ROPE_C // 2))
        k_ref[:, lo] = kv[:, lo].astype(BF16)
        k_ref[:, hi] = kr_bf
        put(v_ref, slice(hh * LANES, (hh + 1) * LANES), kv[:, hi])


def _proj_c(path, x, sh, sc, g, w1, g_q, w_uq, g_kv, w_ukv, tables):
    n, tm, tpb = path.n, path.tm, path.tpb
    w1n = w1.shape[1]
    transposed = path.seq >= TM
    if transposed:
        tiled = lambda width: (jax.ShapeDtypeStruct((path.batch, path.seq // TB, width, TB), BF16),
                               pl.BlockSpec((1, tm // TB, width, TB), lambda i: (i // tpb, i % tpb, 0, 0)))
    else:
        tiled = lambda width: (jax.ShapeDtypeStruct((n, width), BF16), path.rows(width))
    q_shape, q_spec = tiled(H_C * DQK_C_PAD)
    v_shape, v_spec = tiled(H_C * V_C)
    return pl.pallas_call(
        functools.partial(_proj_c_kernel, transposed=transposed, tm=tm),
        grid=(path.steps,),
        in_specs=[path.rows(D_MODEL), path.mod(), path.mod(), _const((1, D_MODEL)),
                  _const((D_MODEL, w1n)), _const((1, Q_LORA)), _const((Q_LORA, H_C * DQK_C_PAD)),
                  _const((1, KV_LORA)), _const((KV_LORA, H_C * (NOPE_C + V_C))),
                  path.table(), path.table(), path.table()],
        out_specs=[q_spec, path.rows(H_C * DQK_C_PAD), v_spec, path.rows(KV_LORA), path.rows(ROPE_C)],
        out_shape=[q_shape, jax.ShapeDtypeStruct((n, H_C * DQK_C_PAD), BF16), v_shape,
                   jax.ShapeDtypeStruct((n, KV_LORA), F32), jax.ShapeDtypeStruct((n, ROPE_C), F32)],
        compiler_params=_params(("arbitrary",)),
        name="proj_c",
    )(x, sh, sc, g, w1, g_q, w_uq, g_kv, w_ukv, *tables)


def _decode_c_kernel(q_ref, kn_ref, vn_ref, kvp_ref, krp_ref, wukv_ref, o_ref, *, seq):
    q_all, kn_all, vn_all = q_ref[0], kn_ref[0], vn_ref[0]
    kv_past = kvp_ref[0, 0].astype(BF16)
    kr_past = krp_ref[0, 0].astype(BF16)
    q_lat, q_rope, s_new = [], [], []
    for h in range(H_C):
        qh = q_all[:, h * DQK_C_PAD:(h + 1) * DQK_C_PAD]
        w_uk = wukv_ref[:, h * (NOPE_C + V_C):h * (NOPE_C + V_C) + NOPE_C]
        q_lat.append(_dot_nt(qh[:, 0:NOPE_C], w_uk).astype(BF16))
        q_rope.append(qh[:, NOPE_C:NOPE_C + ROPE_C])
        s_new.append(_dot_nt(qh, kn_all[:, h * DQK_C_PAD:(h + 1) * DQK_C_PAD]))
    s_past = (_dot_nt(jnp.concatenate(q_lat, axis=0), kv_past)
              + _dot_nt(jnp.concatenate(q_rope, axis=0), kr_past))
    p_past, p_new, l = _softmax_two(s_past, jnp.concatenate(s_new, axis=0))
    lat = _dot(p_past.astype(BF16), kv_past)
    for h in range(H_C):
        rows = slice(h * seq, (h + 1) * seq)
        w_uv = wukv_ref[:, h * (NOPE_C + V_C) + NOPE_C:(h + 1) * (NOPE_C + V_C)]
        o = (_dot(lat[rows].astype(BF16), w_uv)
             + _dot(p_new[rows].astype(BF16), vn_all[:, h * V_C:(h + 1) * V_C])) / l[rows]
        o_ref[0, :, h * V_C:(h + 1) * V_C] = o.astype(BF16)


def _decode_c(q, k, v, cache_kv, cache_kr, w_ukv, jl, batch, seq):
    past = cache_kv.shape[2]
    return pl.pallas_call(
        functools.partial(_decode_c_kernel, seq=seq),
        grid=(batch,),
        in_specs=[pl.BlockSpec((1, seq, H_C * DQK_C_PAD), lambda b: (b, 0, 0)),
                  pl.BlockSpec((1, seq, H_C * DQK_C_PAD), lambda b: (b, 0, 0)),
                  pl.BlockSpec((1, seq, H_C * V_C), lambda b: (b, 0, 0)),
                  pl.BlockSpec((1, 1, past, KV_LORA), lambda b: (jl, b, 0, 0)),
                  pl.BlockSpec((1, 1, past, ROPE_C), lambda b: (jl, b, 0, 0)),
                  _const((KV_LORA, H_C * (NOPE_C + V_C)))],
        out_specs=pl.BlockSpec((1, seq, H_C * V_C), lambda b: (b, 0, 0)),
        out_shape=jax.ShapeDtypeStruct((batch, seq, H_C * V_C), BF16),
        compiler_params=_params(("arbitrary",)),
        name="decode_c",
    )(q, k, v, cache_kv, cache_kr, w_ukv)


def _proj_b_kernel(x_ref, sh_ref, sc_ref, g_ref, w_ref, bg_ref,
                   q_ref, k_ref, kt_ref, v_ref, opre_ref, gates_ref, gt_ref):
    nqk, nv = H_B * DQK_B, H_B * DV_B
    h = (_rms(x_ref[...], g_ref[...]) * (1.0 + sc_ref[0]) + sh_ref[0]).astype(BF16)
    q_ref[...] = _dot(h, w_ref[:, 0:nqk]).astype(BF16)
    k = _dot(h, w_ref[:, nqk:2 * nqk]) * (DQK_B ** -0.5)
    k_ref[...] = k
    kt_ref[0] = k.T
    v_ref[...] = _dot(h, w_ref[:, 2 * nqk:2 * nqk + nv]).astype(BF16)
    opre_ref[...] = _dot(h, w_ref[:, 2 * nqk + nv:2 * nqk + 2 * nv])
    gates = _dot(h, w_ref[:, 2 * nqk + 2 * nv:]) + bg_ref[...]
    gates_ref[...] = gates
    gt_ref[0] = gates.T[0:2 * H_B, :]


def _proj_b(path, x, sh, sc, g, w_in, b_gates):
    n, nqk, nv = path.n, H_B * DQK_B, H_B * DV_B
    groups = path.n // path.table_rows
    tpb = path.tpb
    cols = lambda width: pl.BlockSpec((1, width, path.tm), lambda i: (i // tpb, 0, i % tpb))
    return pl.pallas_call(
        _proj_b_kernel,
        grid=(path.steps,),
        in_specs=[path.rows(D_MODEL), path.mod(), path.mod(), _const((1, D_MODEL)),
                  _const((D_MODEL, w_in.shape[1])), _const((1, LANES))],
        out_specs=[path.rows(nqk), path.rows(nqk), cols(nqk), path.rows(nv), path.rows(nv),
                   path.rows(LANES), cols(2 * H_B)],
        out_shape=[jax.ShapeDtypeStruct((n, nqk), BF16), jax.ShapeDtypeStruct((n, nqk), F32),
                   jax.ShapeDtypeStruct((groups, nqk, path.table_rows), F32),
                   jax.ShapeDtypeStruct((n, nv), BF16), jax.ShapeDtypeStruct((n, nv), F32),
                   jax.ShapeDtypeStruct((n, LANES), F32),
                   jax.ShapeDtypeStruct((groups, 2 * H_B, path.table_rows), F32)],
        compiler_params=_params(("arbitrary",)),
        name="proj_b",
    )(x, sh, sc, g, w_in, b_gates)


def _mlstm_kernel(q_ref, k_ref, kt_ref, v_ref, g_ref, gt_ref, c0_ref, n0_ref, m0_ref,
                  hs_ref, ct_ref, nt_ref, mt_ref, c_scr, n_scr, m_scr, *, chunk, chunks_per_step):
    step = pl.program_id(1)

    @pl.when(step == 0)
    def _():
        c_scr[...] = c0_ref[0]
        n_scr[...] = n0_ref[0]
        m_scr[...] = m0_ref[0]

    L, G = chunk, chunks_per_step
    row = lax.broadcasted_iota(jnp.int32, (L, L), 0)
    col = lax.broadcasted_iota(jnp.int32, (L, L), 1)
    tril = col <= row
    triu = row <= col
    units = [(c, h) for c in range(G) for h in range(H_B)]
    rows_of = lambda c: slice(c * L, (c + 1) * L)
    qk_cols = lambda h: slice(h * DQK_B, (h + 1) * DQK_B)
    v_cols = lambda h: slice(h * DV_B, (h + 1) * DV_B)
    stack = lambda parts: jnp.stack(parts, axis=0)

    g_col = g_ref[0]
    g_row = gt_ref[0]
    lf_col = jax.nn.log_sigmoid(g_col)
    lf_row = jax.nn.log_sigmoid(g_row)
    ig_col = stack([g_col[rows_of(c), h:h + 1] for c, h in units])
    f_col = stack([lf_col[rows_of(c), H_B + h:H_B + h + 1] for c, h in units])
    ig_row = stack([g_row[h:h + 1, rows_of(c)] for c, h in units])
    f_row = stack([lf_row[H_B + h:H_B + h + 1, rows_of(c)] for c, h in units])
    b_col = jnp.sum(jnp.where(tril, f_row, 0.0), axis=2, keepdims=True)
    b_row = jnp.sum(jnp.where(triu, f_col, 0.0), axis=1, keepdims=True)
    g_tot = jnp.sum(f_row, axis=2, keepdims=True)
    dm = jnp.where(tril, b_col - b_row + ig_row, NEG)
    dm_max = jnp.max(dm, axis=2, keepdims=True)
    r_row = g_tot - b_row + ig_row
    r_col = g_tot - b_col + ig_col
    r_max = jnp.max(r_row, axis=2, keepdims=True)

    m_steps = [m_scr[...]]
    for c in range(G):
        heads = slice(c * H_B, (c + 1) * H_B)
        m_steps.append(jnp.maximum(g_tot[heads] + m_steps[c], r_max[heads]))
    m_cur = jnp.concatenate(m_steps[:G], axis=0)
    m_next = jnp.concatenate(m_steps[1:], axis=0)
    inter = b_col + m_cur
    m_t = jnp.maximum(inter, dm_max)
    a = jnp.exp(inter - m_t)
    floor = jnp.exp(-m_t)
    w_in = jnp.exp(dm - m_t)
    wr_row = jnp.exp(r_row - m_next)
    wr_col = jnp.exp(r_col - m_next)
    decay = jnp.exp(g_tot + m_cur - m_next)

    qk = stack([_dot_nt(q_ref[0, rows_of(c), qk_cols(h)], k_ref[0, rows_of(c), qk_cols(h)].astype(BF16))
                for c, h in units]) * w_in
    qk_sum = jnp.sum(qk, axis=2, keepdims=True)
    c_add, n_add, within = [], [], []
    for u, (c, h) in enumerate(units):
        vh = v_ref[0, rows_of(c), v_cols(h)]
        kw_t = kt_ref[0, qk_cols(h), rows_of(c)] * wr_row[u]
        c_add.append(_dot(kw_t.astype(BF16), vh))
        n_add.append(jnp.sum(k_ref[0, rows_of(c), qk_cols(h)] * wr_col[u], axis=0, keepdims=True))
        within.append(_dot(qk[u].astype(BF16), vh))

    c_cur = [c_scr[h] for h in range(H_B)]
    n_cur = [n_scr[h] for h in range(H_B)]
    for c in range(G):
        heads = slice(c * H_B, (c + 1) * H_B)
        q_heads = [q_ref[0, rows_of(c), qk_cols(h)] for h in range(H_B)]
        from_memory = [_dot(q_heads[h], c_cur[h].astype(BF16)) for h in range(H_B)]
        qn = jnp.sum(stack(q_heads).astype(F32) * stack(n_cur), axis=2, keepdims=True)
        den = a[heads] * qn + qk_sum[heads]
        div = jnp.maximum(jnp.abs(den), floor[heads])
        for h in range(H_B):
            u = c * H_B + h
            hs_ref[0, rows_of(c), v_cols(h)] = (a[u] * from_memory[h] + within[u]) / div[h]
            c_cur[h] = decay[u] * c_cur[h] + c_add[u]
            n_cur[h] = decay[u] * n_cur[h] + n_add[u]
    for h in range(H_B):
        c_scr[h] = c_cur[h]
        n_scr[h] = n_cur[h]
    m_scr[...] = m_steps[G]

    @pl.when(step == pl.num_programs(1) - 1)
    def _():
        ct_ref[0] = c_scr[...]
        nt_ref[0] = n_scr[...]
        mt_ref[0] = m_scr[...]


def _mlstm(q, k, kt, v, gates, gt, c0, n0, m0, batch, seq):
    chunk = CHUNK if seq % CHUNK == 0 else seq
    cps = max(1, min(MLSTM_SPAN // chunk, seq // chunk))
    span = chunk * cps
    nqk, nv = H_B * DQK_B, H_B * DV_B
    rows = lambda width: pl.BlockSpec((1, span, width), lambda b, s: (b, s, 0))
    cols = lambda height: pl.BlockSpec((1, height, span), lambda b, s: (b, 0, s))
    state = lambda *shape: pl.BlockSpec((1,) + shape, lambda b, s: (b,) + (0,) * len(shape))
    return pl.pallas_call(
        functools.partial(_mlstm_kernel, chunk=chunk, chunks_per_step=cps),
        grid=(batch, seq // span),
        in_specs=[rows(nqk), rows(nqk), cols(nqk), rows(nv), rows(LANES), cols(2 * H_B),
                  state(H_B, DQK_B, DV_B), state(H_B, 1, DQK_B), state(H_B, 1, 1)],
        out_specs=[rows(nv), state(H_B, DQK_B, DV_B), state(H_B, 1, DQK_B), state(H_B, 1, 1)],
        out_shape=[jax.ShapeDtypeStruct((batch, seq, nv), F32),
                   jax.ShapeDtypeStruct((batch, H_B, DQK_B, DV_B), F32),
                   jax.ShapeDtypeStruct((batch, H_B, 1, DQK_B), F32),
                   jax.ShapeDtypeStruct((batch, H_B, 1, 1), F32)],
        scratch_shapes=[pltpu.VMEM((H_B, DQK_B, DV_B), F32), pltpu.VMEM((H_B, 1, DQK_B), F32),
                        pltpu.VMEM((H_B, 1, 1), F32)],
        compiler_params=_params(("arbitrary", "arbitrary")),
        name="mlstm",
    )(q, k, kt, v, gates, gt, c0, n0, m0)


def _tail_kernel(*refs, gated_heads, final):
    if gated_heads:
        hs_ref, opre_ref, gout_ref = refs[:3]
        refs = refs[3:]
        hs, gout = hs_ref[...], gout_ref[...]
        normed = [_rms(hs[:, h * DV_B:(h + 1) * DV_B], gout[:, h * DV_B:(h + 1) * DV_B]) for h in range(H_B)]
        mixed = (jax.nn.sigmoid(opre_ref[...]) * jnp.concatenate(normed, axis=-1)).astype(BF16)
    else:
        mixed = refs[0][...]
        refs = refs[1:]
    (x_ref, gt1_ref, wo_ref, sh_ref, sc_ref, gt_ref, g_ref, win_ref, wout_ref, gfin_ref, y_ref, acc_scr) = refs
    x = x_ref[...] + gt1_ref[0] * _dot(mixed, wo_ref[...])
    h = (_rms(x, g_ref[...]) * (1.0 + sc_ref[0]) + sh_ref[0]).astype(BF16)
    for c in range(D_FF // FF_CHUNK):
        a = _dot(h, win_ref[:, c * FF_CHUNK:(c + 1) * FF_CHUNK])
        b = _dot(h, win_ref[:, D_FF + c * FF_CHUNK:D_FF + (c + 1) * FF_CHUNK])
        act = (a * jax.nn.sigmoid(a) * b).astype(BF16)
        part = _dot(act, wout_ref[c * FF_CHUNK:(c + 1) * FF_CHUNK, :])
        if c == 0:
            acc_scr[...] = part
        else:
            acc_scr[...] += part
    y = x + gt_ref[0] * acc_scr[...]
    y_ref[...] = _rms(y, gfin_ref[...]) if final else y


def _tail(path, mixer_out, x, gt1, w_o, sh, sc, gt, g, w_in, w_out, g_final, final):
    gated_heads = len(mixer_out) == 3
    mixer_specs = [path.rows(D_MODEL)]
    if gated_heads:
        mixer_specs += [path.rows(D_MODEL), _const((1, D_MODEL))]
    return pl.pallas_call(
        functools.partial(_tail_kernel, gated_heads=gated_heads, final=final),
        grid=(path.steps,),
        in_specs=mixer_specs + [path.rows(D_MODEL), path.mod(), _const((D_MODEL, D_MODEL)),
                                path.mod(), path.mod(), path.mod(), _const((1, D_MODEL)),
                                _const((D_MODEL, 2 * D_FF)), _const((D_FF, D_MODEL)), _const((1, D_MODEL))],
        out_specs=path.rows(D_MODEL),
        out_shape=jax.ShapeDtypeStruct((path.n, D_MODEL), F32),
        scratch_shapes=[pltpu.VMEM((path.tm, D_MODEL), F32)],
        compiler_params=_params(("arbitrary",)),
        name="tail_b" if gated_heads else "tail",
    )(*mixer_out, x, gt1, w_o, sh, sc, gt, g, w_in, w_out, g_final)


def _prep_weights(p):
    w = {}
    w['a_qkv'] = p['w_a_qkv'].astype(BF16)
    w['a_o'] = p['w_a_o'].astype(BF16)
    nqk, nv = H_B * DQK_B, H_B * DV_B
    w_b = p['w_b_in']
    gate_pad = jnp.zeros(w_b.shape[:2] + (LANES - 2 * H_B,), w_b.dtype)
    w['b_in'] = jnp.concatenate([w_b, gate_pad], axis=-1).astype(BF16)
    bg = p['b_b_gates']
    w['b_gates'] = jnp.concatenate([bg, jnp.zeros((bg.shape[0], LANES - 2 * H_B), bg.dtype)], axis=-1)[:, None, :]
    w['b_out'] = p['w_b_out'].astype(BF16)
    n_c = p['w_c_dq'].shape[0]
    w_dkv = p['w_c_dkv']
    w['c_1'] = jnp.concatenate([p['w_c_dq'], w_dkv, jnp.zeros((n_c, D_MODEL, LANES - ROPE_C), w_dkv.dtype)],
                               axis=-1).astype(BF16)
    w_uq = p['w_c_uq'].reshape(n_c, Q_LORA, H_C, NOPE_C + ROPE_C)
    w_uq = jnp.concatenate([w_uq, jnp.zeros((n_c, Q_LORA, H_C, DQK_C_PAD - NOPE_C - ROPE_C), w_uq.dtype)], axis=-1)
    w['c_uq'] = w_uq.reshape(n_c, Q_LORA, H_C * DQK_C_PAD).astype(BF16)
    w['c_ukv'] = p['w_c_ukv'].astype(BF16)
    w['c_o'] = p['w_c_o'].astype(BF16)
    w['ffn_in'] = p['w_ffn_in'].astype(BF16)
    w['ffn_out'] = p['w_ffn_out'].astype(BF16)
    return w


def _trunk(x, mods, past, p, w):
    batch, seq, _ = x.shape
    path = _Path(batch, seq)
    n_past = 0 if past is None else past[5].shape[2]
    pos = n_past + jnp.arange(seq, dtype=jnp.int32)
    tab_a = tuple(path.expand_table(t) for t in _rope_tables(pos, ROT_A, D_SUB))
    tab_c = tuple(path.expand_table(t) for t in _rope_tables(pos, ROPE_C, LANES))
    x = x.reshape(path.n, D_MODEL)
    g_final = p['g_final'][None, :]
    n_a = (DEPTH + N_MIXERS - 1) // N_MIXERS
    a_rows = None
    b_c, b_n, b_m, c_kv, c_kr = [], [], [], [], []
    for i in range(DEPTH):
        kind, j = i % N_MIXERS, i // N_MIXERS
        sh1, sc1, gt1, sh2, sc2, gt2 = (path.expand_mod(m) for m in jnp.split(mods[i], 6, axis=-1))
        g1 = p['g_norm1'][i][None, :]
        if kind == 0:
            lambda_init = 0.8 - 0.6 * math.exp(-0.3 * i)
            q, k, v, k_row, v_row = _proj_a(path, x, sh1, sc1, g1, w['a_qkv'][j], tab_a, j, n_a, a_rows)
            a_rows = (k_row, v_row)
            shape3 = (batch, seq, D_MODEL)
            if past is None:
                g_sub_t = jnp.broadcast_to(p['g_a_sub'][:, :, None], p['g_a_sub'].shape + (TB,))
                o = _flash(q, k.reshape(shape3), v, batch, seq, H_A, LANES,
                           diff_args=(p['a_lambda'], g_sub_t, j), lambda_init=lambda_init)
            else:
                o = _decode_a(q.reshape(shape3), k.reshape(shape3), v.reshape(shape3), past[0], past[1],
                              p['a_lambda'], p['g_a_sub'][:, None, :], j, lambda_init, batch, seq)
            mixer_out, w_o = (o.reshape(path.n, D_MODEL),), w['a_o'][j]
        elif kind == 1:
            q, k, kt, v, opre, gates, gt = _proj_b(path, x, sh1, sc1, g1, w['b_in'][j], w['b_gates'][j])
            nqk, nv = H_B * DQK_B, H_B * DV_B
            if path.table_rows != seq:
                kt = kt.reshape(nqk, batch, seq).transpose(1, 0, 2)
                gt = gt.reshape(2 * H_B, batch, seq).transpose(1, 0, 2)
            if past is None:
                c0 = jnp.zeros((batch, H_B, DQK_B, DV_B), F32)
                n0 = jnp.zeros((batch, H_B, 1, DQK_B), F32)
                m0 = jnp.zeros((batch, H_B, 1, 1), F32)
            else:
                c0 = past[2][j]
                n0 = past[3][j][:, :, None, :]
                m0 = past[4][j][:, :, None, None]
            hs, c_t, n_t, m_t = _mlstm(q.reshape(batch, seq, nqk), k.reshape(batch, seq, nqk), kt,
                                       v.reshape(batch, seq, nv), gates.reshape(batch, seq, LANES), gt,
                                       c0, n0, m0, batch, seq)
            b_c.append(c_t)
            b_n.append(n_t[:, :, 0, :])
            b_m.append(m_t[:, :, 0, 0])
            mixer_out, w_o = (hs.reshape(path.n, nv), opre, p['g_b_out'][j][None, :]), w['b_out'][j]
        else:
            q, k, v, kv_row, kr_row = _proj_c(path, x, sh1, sc1, g1, w['c_1'][j], p['g_c_q'][j][None, :],
                                              w['c_uq'][j], p['g_c_kv'][j][None, :], w['c_ukv'][j], tab_c)
            qk_shape = (batch, seq, H_C * DQK_C_PAD)
            v_shape = (batch, seq, H_C * V_C)
            if past is None:
                o = _flash(q, k.reshape(qk_shape), v, batch, seq, H_C, DQK_C_PAD)
            else:
                o = _decode_c(q.reshape(qk_shape), k.reshape(qk_shape), v.reshape(v_shape), past[5], past[6],
                              w['c_ukv'][j], j, batch, seq)
            c_kv.append(kv_row.reshape(batch, seq, KV_LORA))
            c_kr.append(kr_row.reshape(batch, seq, ROPE_C))
            mixer_out, w_o = (o.reshape(path.n, D_MODEL),), w['c_o'][j]
        x = _tail(path, mixer_out, x, gt1, w_o, sh2, sc2, gt2, p['g_norm2'][i][None, :],
                  w['ffn_in'][i], w['ffn_out'][i], g_final, final=(i == DEPTH - 1))
    y = x.reshape(batch, seq, D_MODEL)
    a_k, a_v = (r.reshape(n_a, batch, seq, H_A, 2 * D_SUB) for r in a_rows)
    return y, (a_k, a_v, jnp.stack(b_c), jnp.stack(b_n), jnp.stack(b_m), jnp.stack(c_kv), jnp.stack(c_kr))


def kernel(x_prompt, x_sample, c_prompt, c_sample, cache_a_k, cache_a_v, state_b_c, state_b_n, state_b_m, cache_c_kv, cache_c_kr, w_ada, b_ada, g_norm1, g_norm2, w_a_qkv, a_lambda, g_a_sub, w_a_o, w_b_in, b_b_gates, g_b_out, w_b_out, w_c_dq, g_c_q, w_c_uq, w_c_dkv, g_c_kv, w_c_ukv, w_c_o, w_ffn_in, w_ffn_out, g_final):
    p = dict(g_norm1=g_norm1, g_norm2=g_norm2, w_a_qkv=w_a_qkv, a_lambda=a_lambda, g_a_sub=g_a_sub,
             w_a_o=w_a_o, w_b_in=w_b_in, b_b_gates=b_b_gates, g_b_out=g_b_out, w_b_out=w_b_out,
             w_c_dq=w_c_dq, g_c_q=g_c_q, w_c_uq=w_c_uq, w_c_dkv=w_c_dkv, g_c_kv=g_c_kv, w_c_ukv=w_c_ukv,
             w_c_o=w_c_o, w_ffn_in=w_ffn_in, w_ffn_out=w_ffn_out, g_final=g_final)
    w = _prep_weights(p)
    nb_p = x_prompt.shape[0]
    mods = _ada(jnp.concatenate([c_prompt, c_sample], axis=0), w_ada, b_ada)
    y_prompt, sp = _trunk(x_prompt, mods[:, :nb_p], None, p, w)
    past = (cache_a_k.reshape(-1, 2 * D_SUB), cache_a_v.reshape(-1, 2 * D_SUB),
            state_b_c, state_b_n, state_b_m, cache_c_kv, cache_c_kr)
    y_sample, ss = _trunk(x_sample, mods[:, nb_p:], past, p, w)
    return (y_prompt, y_sample) + sp + ss
```

```python
import functools
import math

import jax
import jax.numpy as jnp
from jax import lax
from jax.experimental import pallas as pl
from jax.experimental.pallas import tpu as pltpu

F32 = jnp.float32
BF16 = jnp.bfloat16

D_MODEL = 1024
DEPTH = 4
CHUNK = 64
N_MIXERS = 3
ROPE_THETA = 500000.0
EPS = 1e-6
D_SUB = 64
H_A = D_MODEL // (2 * D_SUB)
ROT_A = D_SUB // 4
H_B = 4
DQK_B = D_MODEL // (2 * H_B)
DV_B = D_MODEL // H_B
H_C = 8
NOPE_C = 128
ROPE_C = 64
V_C = 128
Q_LORA = 384
KV_LORA = 256
D_FF = -(-8 * D_MODEL // (3 * 256)) * 256

LANES = 128
BF16_ROWS = 16
DQK_C_PAD = 2 * LANES
NEG = -1e30
VMEM_LIMIT = 56 * 1024 * 1024
TM = 512
TB = 256
HEAD_GROUP = 4
FF_CHUNK = 256
MLSTM_SPAN = 256
LOG2E = math.log2(math.e)
QSCALE_A = D_SUB ** -0.5 * LOG2E
QSCALE_C = (NOPE_C + ROPE_C) ** -0.5 * LOG2E


def _params(sem):
    return pltpu.CompilerParams(dimension_semantics=sem, vmem_limit_bytes=VMEM_LIMIT)


def _const(shape):
    nd = len(shape)
    return pl.BlockSpec(shape, lambda *_: (0,) * nd, pipeline_mode=pl.Buffered(1))


def _layer(shape, j):
    nd = len(shape)
    return pl.BlockSpec((None,) + tuple(shape), lambda *_: (j,) + (0,) * nd, pipeline_mode=pl.Buffered(1))


def _rms(x, g):
    return x * lax.rsqrt(jnp.mean(x * x, axis=-1, keepdims=True) + EPS) * g


def _dot(a, b):
    return jnp.dot(a, b, preferred_element_type=F32)


def _dot_nt(a, b):
    return lax.dot_general(a, b, (((1,), (1,)), ((), ())), preferred_element_type=F32)


def _rope_cols(c, cos, sin_lo, sin_hi, half):
    return c * cos + pltpu.roll(c, LANES - half, 1) * sin_lo + pltpu.roll(c, half, 1) * sin_hi


def _ada_kernel(c_ref, w_ref, b_ref, o_ref):
    c = c_ref[...]
    s = (c * jax.nn.sigmoid(c)).astype(BF16)
    o_ref[0] = _dot(s, w_ref[0].astype(BF16)) + b_ref[0]


def _ada(c_all, w_ada, b_ada):
    n = c_all.shape[0]
    tn = 1536
    return pl.pallas_call(
        _ada_kernel,
        grid=(DEPTH, 6 * D_MODEL // tn),
        in_specs=[pl.BlockSpec((n, D_MODEL), lambda i, j: (0, 0)),
                  pl.BlockSpec((1, D_MODEL, tn), lambda i, j: (i, 0, j)),
                  pl.BlockSpec((1, 1, tn), lambda i, j: (i, 0, j))],
        out_specs=pl.BlockSpec((1, n, tn), lambda i, j: (i, 0, j)),
        out_shape=jax.ShapeDtypeStruct((DEPTH, n, 6 * D_MODEL), F32),
        compiler_params=_params(("arbitrary", "arbitrary")),
        name="ada",
    )(c_all, w_ada, b_ada.reshape(DEPTH, 1, 6 * D_MODEL))


class _Path:
    def __init__(self, batch, seq):
        self.batch, self.seq = batch, seq
        self.n = batch * seq
        if seq >= TM:
            self.tm, self.tpb, self.mod_rows = TM, seq // TM, 1
        else:
            self.tm, self.tpb, self.mod_rows = self.n, 1, self.n
        self.steps = self.n // self.tm
        self.table_rows = self.tpb * self.tm

    def rows(self, width):
        return pl.BlockSpec((self.tm, width), lambda i: (i, 0))

    def mod(self, which):
        tpb = self.tpb
        layer, part = which
        return pl.BlockSpec((None, None, self.mod_rows, D_MODEL), lambda i: (layer, i // tpb, 0, part))

    def table(self):
        tpb = self.tpb
        return pl.BlockSpec((self.tm, LANES), lambda i: (i % tpb, 0))

    def expand_mod(self, m):
        if self.mod_rows == 1:
            return m[:, :, None, :]
        return jnp.repeat(m, self.seq, axis=1)[:, None]

    def expand_table(self, t):
        if self.table_rows == self.seq:
            return t
        return jnp.tile(t, (self.table_rows // self.seq, 1))


def _rope_tables(pos, rot, period):
    half = rot // 2
    inv = jnp.power(jnp.float32(ROPE_THETA), -jnp.arange(half, dtype=jnp.float32) * (2.0 / rot))
    ang = pos.astype(jnp.float32)[:, None] * inv[None, :]
    cos, sin = jnp.cos(ang), jnp.sin(ang)
    n = pos.shape[0]
    reps = LANES // period
    pad1 = jnp.ones((n, period - rot), F32)
    pad0 = jnp.zeros((n, period - rot), F32)
    zh = jnp.zeros((n, half), F32)
    cos_t = jnp.tile(jnp.concatenate([cos, cos, pad1], axis=1), (1, reps))
    sin_lo = jnp.tile(jnp.concatenate([-sin, zh, pad0], axis=1), (1, reps))
    sin_hi = jnp.tile(jnp.concatenate([zh, sin, pad0], axis=1), (1, reps))
    return cos_t, sin_lo, sin_hi


def _rope_tables_t(pos, rot):
    half = rot // 2
    inv = jnp.power(jnp.float32(ROPE_THETA), -jnp.arange(half, dtype=jnp.float32) * (2.0 / rot))
    ang = pos.astype(jnp.float32)[:, None] * inv[None, :]
    return jnp.cos(ang).T, jnp.sin(ang).T


def _proj_a_kernel(*refs, transposed, aliased, tm, slot):
    refs = refs[2:] if aliased else refs
    (x_ref, sh_ref, sc_ref, g_ref, w_ref, cos_ref, slo_ref, shi_ref,
     q_ref, k_ref, v_ref, krow_all, vrow_all) = refs
    if aliased:
        krow_ref, vrow_ref = krow_all.at[0], vrow_all.at[0]
    else:
        krow_ref, vrow_ref = krow_all.at[slot], vrow_all.at[slot]
        for other in range(krow_all.shape[0]):
            if other != slot:
                krow_all[other] = jnp.zeros(krow_all.shape[1:], F32)
                vrow_all[other] = jnp.zeros(vrow_all.shape[1:], F32)
    h = (_rms(x_ref[...], g_ref[...]) * (1.0 + sc_ref[...]) + sh_ref[...]).astype(BF16)
    cos, slo, shi = cos_ref[...], slo_ref[...], shi_ref[...]
    q = _dot(h, w_ref[:, 0:D_MODEL])
    k = _dot(h, w_ref[:, D_MODEL:2 * D_MODEL])
    v = _dot(h, w_ref[:, 2 * D_MODEL:3 * D_MODEL])
    for j in range(H_A):
        cs = slice(j * LANES, (j + 1) * LANES)
        qj = _rope_cols(q[:, cs], cos, slo, shi, ROT_A // 2) * QSCALE_A
        kj = _rope_cols(k[:, cs], cos, slo, shi, ROT_A // 2)
        vj = v[:, cs]
        krow_ref[pl.ds(j, tm, stride=H_A), :] = kj
        vrow_ref[pl.ds(j, tm, stride=H_A), :] = vj
        k_ref[:, cs] = kj.astype(BF16)
        if transposed:
            qt, vt = qj.T.astype(BF16), vj.T.astype(BF16)
            for t in range(tm // TB):
                q_ref[0, t, cs, :] = qt[:, t * TB:(t + 1) * TB]
                v_ref[0, t, cs, :] = vt[:, t * TB:(t + 1) * TB]
        else:
            q_ref[:, cs] = qj.astype(BF16)
            v_ref[:, cs] = vj.astype(BF16)


def _proj_a(path, x, mods, layer, g, w_qkv, tables, layer_slot, n_slots, prev_rows):
    n, tm, tpb, steps = path.n, path.tm, path.tpb, path.steps
    transposed = path.seq >= TM
    aliased = prev_rows is not None
    bf = jax.ShapeDtypeStruct((n, D_MODEL), BF16)
    if transposed:
        bft = jax.ShapeDtypeStruct((path.batch, path.seq // TB, D_MODEL, TB), BF16)
        tspec = pl.BlockSpec((1, tm // TB, D_MODEL, TB), lambda i: (i // tpb, i % tpb, 0, 0))
        qv_shape, qv_spec = bft, tspec
    else:
        qv_shape, qv_spec = bf, path.rows(D_MODEL)
    rows_shape = jax.ShapeDtypeStruct((n_slots, n * H_A, LANES), F32)
    if aliased:
        rows_spec = pl.BlockSpec((1, tm * H_A, LANES), lambda i: (layer_slot, i, 0))
    else:
        rows_spec = pl.BlockSpec((n_slots, tm * H_A, LANES), lambda i: (0, i, 0))
    in_specs = [path.rows(D_MODEL), path.mod((layer, 0)), path.mod((layer, 1)), _const((1, D_MODEL)),
                _layer((D_MODEL, 3 * D_MODEL), layer_slot), path.table(), path.table(), path.table()]
    args = [x, mods, mods, g, w_qkv, *tables]
    aliases = {}
    if aliased:
        in_specs = [pl.BlockSpec(memory_space=pl.ANY)] * 2 + in_specs
        args = list(prev_rows) + args
        aliases = {0: 3, 1: 4}
    return pl.pallas_call(
        functools.partial(_proj_a_kernel, transposed=transposed, aliased=aliased, tm=tm, slot=layer_slot),
        grid=(steps,),
        in_specs=in_specs,
        out_specs=[qv_spec, path.rows(D_MODEL), qv_spec, rows_spec, rows_spec],
        out_shape=[qv_shape, bf, qv_shape, rows_shape, rows_shape],
        input_output_aliases=aliases,
        compiler_params=_params(("arbitrary",)),
        name="proj_a",
    )(*args)


def _lambda_full(lam_ref, lambda_init):
    lv = lam_ref[0]
    d1 = jnp.sum(lv[0:1] * lv[1:2], axis=-1, keepdims=True)
    d2 = jnp.sum(lv[2:3] * lv[3:4], axis=-1, keepdims=True)
    return jnp.exp(d1) - jnp.exp(d2) + lambda_init


def _split_subheads(q):
    lane = lax.broadcasted_iota(jnp.int32, q.shape, 1)
    zero = jnp.zeros_like(q)
    return [jnp.where(lane < D_SUB, q, zero), jnp.where(lane >= D_SUB, q, zero)]


def _diff_combine(o1, o2, lam, g_sub, lambda_init):
    o = o1 - lam * o2
    return _rms(o, g_sub) * (1.0 - lambda_init)


def _flash_kernel(*refs, diff, lambda_init, n_tiles, heads, dqk):
    if diff:
        qt_ref, k_ref, vt_ref, lam_ref, gsub_ref, o_ref, m_scr, acc_scr, sa_scr, sb_scr = refs
    else:
        qt_ref, k_ref, vt_ref, o_ref, m_scr, acc_scr, sa_scr, sb_scr = refs
    n_sub = 2 if diff else 1
    key = lax.broadcasted_iota(jnp.int32, (TB, TB), 0)
    qry = lax.broadcasted_iota(jnp.int32, (TB, TB), 1)
    visible = (key // CHUNK) <= (qry // CHUNK)
    ones_rows = jnp.ones((BF16_ROWS, TB), BF16)

    def normalised(c):
        acc = acc_scr[c]
        return acc[0:LANES] / acc[LANES:LANES + 1]

    def chain_queries(i):
        qs = []
        for h in range(heads):
            qt = qt_ref[0, i, h * dqk:(h + 1) * dqk, :]
            if diff:
                dim = lax.broadcasted_iota(jnp.int32, qt.shape, 0)
                zero = jnp.zeros_like(qt)
                qs += [jnp.where(dim < D_SUB, qt, zero), jnp.where(dim >= D_SUB, qt, zero)]
            else:
                qs.append(qt)
        return qs

    def score_block(qs, j, s_scr):
        rows = pl.ds(pl.multiple_of(j * TB, TB), TB)
        for h in range(heads):
            kb = k_ref[0, rows, h * dqk:(h + 1) * dqk]
            for s in range(n_sub):
                s_scr[h * n_sub + s] = _dot(kb, qs[h * n_sub + s])

    score_block(chain_queries(0), 0, sa_scr)

    def q_tile(i, carry):
        qs = chain_queries(i)
        scores = functools.partial(score_block, qs)
        m_scr[...] = jnp.full(m_scr.shape, NEG, F32)
        acc_scr[...] = jnp.zeros(acc_scr.shape, F32)

        def update(j, s_scr, masked):
            for h in range(heads):
                vb = jnp.concatenate([vt_ref[0, j, h * LANES:(h + 1) * LANES, :], ones_rows], axis=0)
                for s in range(n_sub):
                    c = h * n_sub + s
                    st = jnp.where(visible, s_scr[c], NEG) if masked else s_scr[c]
                    m_prev = m_scr[c]
                    m_new = jnp.maximum(m_prev, jnp.max(st, axis=0, keepdims=True))
                    p = jnp.exp2(st - m_new)
                    acc_scr[c] = jnp.exp2(m_prev - m_new) * acc_scr[c] + _dot(vb, p.astype(BF16))
                    m_scr[c] = m_new

        def pair(t, c):
            scores(2 * t + 1, sb_scr)
            update(2 * t, sa_scr, False)
            scores(2 * t + 2, sa_scr)
            update(2 * t + 1, sb_scr, False)
            return c

        lax.fori_loop(0, i // 2, pair, 0)

        @pl.when(i % 2 == 0)
        def _():
            update(i, sa_scr, True)

        @pl.when(i % 2 == 1)
        def _():
            scores(i, sb_scr)
            update(i - 1, sa_scr, False)
            update(i, sb_scr, True)

        score_block(chain_queries(jnp.minimum(i + 1, n_tiles - 1)), 0, sa_scr)
        out_rows = pl.ds(pl.multiple_of(i * TB, TB), TB)
        for h in range(heads):
            if diff:
                ot = normalised(2 * h) - _lambda_full(lam_ref, lambda_init) * normalised(2 * h + 1)
                ot = (ot * lax.rsqrt(jnp.mean(ot * ot, axis=0, keepdims=True) + EPS) * gsub_ref[0]
                      * (1.0 - lambda_init))
            else:
                ot = normalised(h)
            o_ref[0, out_rows, h * LANES:(h + 1) * LANES] = ot.T.astype(BF16)
        return carry

    lax.fori_loop(0, n_tiles, q_tile, 0)


def _flash(qt, k, vt, batch, seq, heads, dqk, diff_args=None, lambda_init=0.0):
    diff = diff_args is not None
    hg = HEAD_GROUP
    n_tiles = seq // TB
    chains = hg * (2 if diff else 1)
    in_specs = [pl.BlockSpec((1, n_tiles, hg * dqk, TB), lambda b, g: (b, 0, g, 0)),
                pl.BlockSpec((1, seq, hg * dqk), lambda b, g: (b, 0, g)),
                pl.BlockSpec((1, n_tiles, hg * LANES, TB), lambda b, g: (b, 0, g, 0))]
    args = [qt, k, vt]
    if diff:
        lam, g_sub_t, jl = diff_args
        in_specs += [pl.BlockSpec((1, 4, D_SUB), lambda b, g: (jl, 0, 0)),
                     pl.BlockSpec((1, LANES, TB), lambda b, g: (jl, 0, 0))]
        args += [lam, g_sub_t]
    return pl.pallas_call(
        functools.partial(_flash_kernel, diff=diff, lambda_init=lambda_init, n_tiles=n_tiles, heads=hg, dqk=dqk),
        grid=(batch, heads // hg),
        in_specs=in_specs,
        out_specs=pl.BlockSpec((1, seq, hg * LANES), lambda b, g: (b, 0, g)),
        out_shape=jax.ShapeDtypeStruct((batch, seq, heads * LANES), BF16),
        scratch_shapes=[pltpu.VMEM((chains, 1, TB), F32),
                        pltpu.VMEM((chains, LANES + BF16_ROWS, TB), F32),
                        pltpu.VMEM((chains, TB, TB), F32), pltpu.VMEM((chains, TB, TB), F32)],
        compiler_params=_params(("arbitrary", "arbitrary")),
        name="flash_diff" if diff else "flash_mla",
    )(*args)


def _softmax_two(s_past, s_new):
    m = jnp.maximum(jnp.max(s_past, axis=-1, keepdims=True), jnp.max(s_new, axis=-1, keepdims=True))
    p_past = jnp.exp2(s_past - m)
    p_new = jnp.exp2(s_new - m)
    l = jnp.sum(p_past, axis=-1, keepdims=True) + jnp.sum(p_new, axis=-1, keepdims=True)
    return p_past, p_new, l


def _decode_a_kernel(q_ref, kn_ref, vn_ref, kp_ref, vp_ref, lam_ref, gsub_ref, o_ref, *, lambda_init, past, seq):
    lam = _lambda_full(lam_ref, lambda_init)
    q_all, kn_all, vn_all = q_ref[0], kn_ref[0], vn_ref[0]
    s_past, s_new = [], []
    for h in range(H_A):
        cs = slice(h * LANES, (h + 1) * LANES)
        q2 = jnp.concatenate(_split_subheads(q_all[:, cs]), axis=0)
        kp = kp_ref[pl.ds(h, past, stride=H_A), :].astype(BF16)
        s_past.append(_dot_nt(q2, kp))
        s_new.append(_dot_nt(q2, kn_all[:, cs]))
    for h in range(H_A):
        cs = slice(h * LANES, (h + 1) * LANES)
        vp = vp_ref[pl.ds(h, past, stride=H_A), :].astype(BF16)
        p_past, p_new, l = _softmax_two(s_past[h], s_new[h])
        o2 = (_dot(p_past.astype(BF16), vp) + _dot(p_new.astype(BF16), vn_all[:, cs])) / l
        o = _diff_combine(o2[0:seq], o2[seq:2 * seq], lam, gsub_ref[0], lambda_init)
        o_ref[0, :, cs] = o.astype(BF16)


def _decode_a(q, k, v, cache_k, cache_v, lam, g_sub, jl, lambda_init, batch, seq):
    past = cache_k.shape[0] // (H_A * batch * lam.shape[0])
    new = pl.BlockSpec((1, seq, D_MODEL), lambda b: (b, 0, 0))
    old = pl.BlockSpec((past * H_A, LANES), lambda b: (jl * batch + b, 0))
    return pl.pallas_call(
        functools.partial(_decode_a_kernel, lambda_init=lambda_init, past=past, seq=seq),
        grid=(batch,),
        in_specs=[new, new, new, old, old,
                  pl.BlockSpec((1, 4, D_SUB), lambda b: (jl, 0, 0)),
                  pl.BlockSpec((1, 1, LANES), lambda b: (jl, 0, 0))],
        out_specs=new,
        out_shape=jax.ShapeDtypeStruct((batch, seq, D_MODEL), BF16),
        compiler_params=_params(("arbitrary",)),
        name="decode_a",
    )(q, k, v, cache_k, cache_v, lam, g_sub)


def _proj_c_kernel(x_ref, sh_ref, sc_ref, g_ref, w1_ref, gq_ref, wuq_ref, gkv_ref, wukv_ref,
                   cos_ref, slo_ref, shi_ref, q_ref, k_ref, v_ref, kvrow_ref, krrow_ref):
    h = (_rms(x_ref[...], g_ref[...]) * (1.0 + sc_ref[...]) + sh_ref[...]).astype(BF16)
    cos, slo, shi = cos_ref[...], slo_ref[...], shi_ref[...]
    d = _dot(h, w1_ref[...])
    qlat = _rms(d[:, 0:Q_LORA], gq_ref[...]).astype(BF16)
    kv_row = _rms(d[:, Q_LORA:Q_LORA + KV_LORA], gkv_ref[...])
    kr = _rope_cols(d[:, Q_LORA + KV_LORA:], cos, slo, shi, ROPE_C // 2)
    kvrow_ref[...] = kv_row
    krrow_ref[...] = kr[:, 0:ROPE_C]
    kr_bf = kr.astype(BF16)
    q = _dot(qlat, wuq_ref[...]) * QSCALE_C
    kv = _dot(kv_row.astype(BF16), wukv_ref[...])

    for hh in range(H_C):
        lo = slice(hh * DQK_C_PAD, hh * DQK_C_PAD + LANES)
        hi = slice(hh * DQK_C_PAD + LANES, (hh + 1) * DQK_C_PAD)
        q_ref[:, lo] = q[:, lo].astype(BF16)
        q_ref[:, hi] = _rope_cols(q[:, hi], cos, slo, shi, ROPE_C // 2).astype(BF16)
        k_ref[:, lo] = kv[:, lo].astype(BF16)
        k_ref[:, hi] = kr_bf
        v_ref[:, hh * LANES:(hh + 1) * LANES] = kv[:, hi].astype(BF16)


def _proj_c_t_kernel(x_ref, sh_ref, sc_ref, g_ref, w1_ref, gq_ref, wuqt_ref, gkv_ref, wuk_ref, wuvt_ref,
                     cos_ref, slo_ref, shi_ref, cost_ref, sint_ref,
                     q_ref, k_ref, v_ref, kvrow_ref, krrow_ref, *, tm):
    h = (_rms(x_ref[...], g_ref[...]) * (1.0 + sc_ref[...]) + sh_ref[...]).astype(BF16)
    d = _dot(h, w1_ref[...])
    qlat = _rms(d[:, 0:Q_LORA], gq_ref[...])
    kv_row = _rms(d[:, Q_LORA:Q_LORA + KV_LORA], gkv_ref[...])
    kr = _rope_cols(d[:, Q_LORA + KV_LORA:], cos_ref[...], slo_ref[...], shi_ref[...], ROPE_C // 2)
    kvrow_ref[...] = kv_row
    krrow_ref[...] = kr[:, 0:ROPE_C]
    kr_bf = kr.astype(BF16)
    q_t = _dot(wuqt_ref[...], qlat.T.astype(BF16)) * QSCALE_C
    v_t = _dot(wuvt_ref[...], kv_row.T.astype(BF16))
    k_nope = _dot(kv_row.astype(BF16), wuk_ref[...])
    cos_t, sin_t = cost_ref[...], sint_ref[...]
    half = ROPE_C // 2
    for hh in range(H_C):
        base = hh * DQK_C_PAD
        x1 = q_t[base + NOPE_C:base + NOPE_C + half]
        x2 = q_t[base + NOPE_C + half:base + NOPE_C + ROPE_C]
        q_h = jnp.concatenate([q_t[base:base + NOPE_C], x1 * cos_t - x2 * sin_t, x2 * cos_t + x1 * sin_t,
                               q_t[base + NOPE_C + ROPE_C:base + DQK_C_PAD]], axis=0).astype(BF16)
        v_h = v_t[hh * V_C:(hh + 1) * V_C].astype(BF16)
        for t in range(tm // TB):
            q_ref[0, t, base:base + DQK_C_PAD, :] = q_h[:, t * TB:(t + 1) * TB]
            v_ref[0, t, hh * V_C:(hh + 1) * V_C, :] = v_h[:, t * TB:(t + 1) * TB]
        k_ref[:, base:base + NOPE_C] = k_nope[:, hh * NOPE_C:(hh + 1) * NOPE_C].astype(BF16)
        k_ref[:, base + NOPE_C:base + DQK_C_PAD] = kr_bf


def _proj_c(path, x, mods, layer, g, j, w, g_q, g_kv, tables, tables_t):
    n, tm, tpb = path.n, path.tm, path.tpb
    w1n = w['c_1'].shape[2]
    transposed = path.seq >= TM
    common_in = [path.rows(D_MODEL), path.mod((layer, 0)), path.mod((layer, 1)), _const((1, D_MODEL)),
                 _layer((D_MODEL, w1n), j), _const((1, Q_LORA))]
    row_outs = [path.rows(KV_LORA), path.rows(ROPE_C)]
    row_shapes = [jax.ShapeDtypeStruct((n, KV_LORA), F32), jax.ShapeDtypeStruct((n, ROPE_C), F32)]
    k_shape = jax.ShapeDtypeStruct((n, H_C * DQK_C_PAD), BF16)
    if transposed:
        tiled = lambda width: (jax.ShapeDtypeStruct((path.batch, path.seq // TB, width, TB), BF16),
                               pl.BlockSpec((1, tm // TB, width, TB), lambda i: (i // tpb, i % tpb, 0, 0)))
        q_shape, q_spec = tiled(H_C * DQK_C_PAD)
        v_shape, v_spec = tiled(H_C * V_C)
        table_t = pl.BlockSpec((ROPE_C // 2, tm), lambda i: (0, i % tpb))
        return pl.pallas_call(
            functools.partial(_proj_c_t_kernel, tm=tm),
            grid=(path.steps,),
            in_specs=common_in + [_layer((H_C * DQK_C_PAD, Q_LORA), j), _const((1, KV_LORA)),
                                  _layer((KV_LORA, H_C * NOPE_C), j), _layer((H_C * V_C, KV_LORA), j),
                                  path.table(), path.table(), path.table(), table_t, table_t],
            out_specs=[q_spec, path.rows(H_C * DQK_C_PAD), v_spec] + row_outs,
            out_shape=[q_shape, k_shape, v_shape] + row_shapes,
            compiler_params=_params(("arbitrary",)),
            name="proj_c",
        )(x, mods, mods, g, w['c_1'], g_q, w['c_uq_t'], g_kv, w['c_uk'], w['c_uv_t'], *tables, *tables_t)
    w_uq, w_ukv = w['c_uq'], w['c_ukv']
    tiled = lambda width: (jax.ShapeDtypeStruct((n, width), BF16), path.rows(width))
    q_shape, q_spec = tiled(H_C * DQK_C_PAD)
    v_shape, v_spec = tiled(H_C * V_C)
    return pl.pallas_call(
        _proj_c_kernel,
        grid=(path.steps,),
        in_specs=common_in + [_layer((Q_LORA, H_C * DQK_C_PAD), j), _const((1, KV_LORA)),
                              _layer((KV_LORA, H_C * (NOPE_C + V_C)), j),
                              path.table(), path.table(), path.table()],
        out_specs=[q_spec, path.rows(H_C * DQK_C_PAD), v_spec] + row_outs,
        out_shape=[q_shape, k_shape, v_shape] + row_shapes,
        compiler_params=_params(("arbitrary",)),
        name="proj_c",
    )(x, mods, mods, g, w['c_1'], g_q, w_uq, g_kv, w_ukv, *tables)


def _decode_c_kernel(q_ref, kn_ref, vn_ref, kvp_ref, krp_ref, wukv_ref, o_ref, *, seq):
    q_all, kn_all, vn_all = q_ref[0], kn_ref[0], vn_ref[0]
    kv_past = kvp_ref[0, 0].astype(BF16)
    kr_past = krp_ref[0, 0].astype(BF16)
    q_lat, q_rope, s_new = [], [], []
    for h in range(H_C):
        qh = q_all[:, h * DQK_C_PAD:(h + 1) * DQK_C_PAD]
        w_uk = wukv_ref[:, h * (NOPE_C + V_C):h * (NOPE_C + V_C) + NOPE_C]
        q_lat.append(_dot_nt(qh[:, 0:NOPE_C], w_uk).astype(BF16))
        q_rope.append(qh[:, NOPE_C:NOPE_C + ROPE_C])
        s_new.append(_dot_nt(qh, kn_all[:, h * DQK_C_PAD:(h + 1) * DQK_C_PAD]))
    s_past = (_dot_nt(jnp.concatenate(q_lat, axis=0), kv_past)
              + _dot_nt(jnp.concatenate(q_rope, axis=0), kr_past))
    p_past, p_new, l = _softmax_two(s_past, jnp.concatenate(s_new, axis=0))
    lat = _dot(p_past.astype(BF16), kv_past)
    for h in range(H_C):
        rows = slice(h * seq, (h + 1) * seq)
        w_uv = wukv_ref[:, h * (NOPE_C + V_C) + NOPE_C:(h + 1) * (NOPE_C + V_C)]
        o = (_dot(lat[rows].astype(BF16), w_uv)
             + _dot(p_new[rows].astype(BF16), vn_all[:, h * V_C:(h + 1) * V_C])) / l[rows]
        o_ref[0, :, h * V_C:(h + 1) * V_C] = o.astype(BF16)


def _decode_c(q, k, v, cache_kv, cache_kr, w_ukv, jl, batch, seq):
    past = cache_kv.shape[2]
    return pl.pallas_call(
        functools.partial(_decode_c_kernel, seq=seq),
        grid=(batch,),
        in_specs=[pl.BlockSpec((1, seq, H_C * DQK_C_PAD), lambda b: (b, 0, 0)),
                  pl.BlockSpec((1, seq, H_C * DQK_C_PAD), lambda b: (b, 0, 0)),
                  pl.BlockSpec((1, seq, H_C * V_C), lambda b: (b, 0, 0)),
                  pl.BlockSpec((1, 1, past, KV_LORA), lambda b: (jl, b, 0, 0)),
                  pl.BlockSpec((1, 1, past, ROPE_C), lambda b: (jl, b, 0, 0)),
                  _layer((KV_LORA, H_C * (NOPE_C + V_C)), jl)],
        out_specs=pl.BlockSpec((1, seq, H_C * V_C), lambda b: (b, 0, 0)),
        out_shape=jax.ShapeDtypeStruct((batch, seq, H_C * V_C), BF16),
        compiler_params=_params(("arbitrary",)),
        name="decode_c",
    )(q, k, v, cache_kv, cache_kr, w_ukv)


def _proj_b_kernel(x_ref, sh_ref, sc_ref, g_ref, w_ref, bg_ref,
                   q_ref, k_ref, kt_ref, v_ref, opre_ref, gates_ref, gt_ref):
    nqk, nv = H_B * DQK_B, H_B * DV_B
    h = (_rms(x_ref[...], g_ref[...]) * (1.0 + sc_ref[...]) + sh_ref[...]).astype(BF16)
    q_ref[...] = _dot(h, w_ref[:, 0:nqk]).astype(BF16)
    k = _dot(h, w_ref[:, nqk:2 * nqk]) * (DQK_B ** -0.5)
    k_ref[...] = k
    kt_ref[0] = k.T
    v_ref[...] = _dot(h, w_ref[:, 2 * nqk:2 * nqk + nv]).astype(BF16)
    opre_ref[...] = _dot(h, w_ref[:, 2 * nqk + nv:2 * nqk + 2 * nv])
    gates = _dot(h, w_ref[:, 2 * nqk + 2 * nv:]) + bg_ref[...]
    gates_ref[...] = gates
    gt_ref[0] = gates.T[0:2 * H_B, :]


def _proj_b(path, x, mods, layer, g, j, w_in, b_gates):
    n, nqk, nv = path.n, H_B * DQK_B, H_B * DV_B
    groups = path.n // path.table_rows
    tpb = path.tpb
    cols = lambda width: pl.BlockSpec((1, width, path.tm), lambda i: (i // tpb, 0, i % tpb))
    return pl.pallas_call(
        _proj_b_kernel,
        grid=(path.steps,),
        in_specs=[path.rows(D_MODEL), path.mod((layer, 0)), path.mod((layer, 1)), _const((1, D_MODEL)),
                  _layer((D_MODEL, w_in.shape[2]), j), _const((1, LANES))],
        out_specs=[path.rows(nqk), path.rows(nqk), cols(nqk), path.rows(nv), path.rows(nv),
                   path.rows(LANES), cols(2 * H_B)],
        out_shape=[jax.ShapeDtypeStruct((n, nqk), BF16), jax.ShapeDtypeStruct((n, nqk), F32),
                   jax.ShapeDtypeStruct((groups, nqk, path.table_rows), F32),
                   jax.ShapeDtypeStruct((n, nv), BF16), jax.ShapeDtypeStruct((n, nv), F32),
                   jax.ShapeDtypeStruct((n, LANES), F32),
                   jax.ShapeDtypeStruct((groups, 2 * H_B, path.table_rows), F32)],
        compiler_params=_params(("arbitrary",)),
        name="proj_b",
    )(x, mods, mods, g, w_in, b_gates)


def _mlstm_kernel(q_ref, k_ref, kt_ref, v_ref, g_ref, gt_ref, c0_ref, n0_ref, m0_ref,
                  hs_ref, ct_ref, nt_ref, mt_ref, c_scr, n_scr, m_scr, *, chunk, chunks_per_step):
    step = pl.program_id(1)

    @pl.when(step == 0)
    def _():
        c_scr[...] = c0_ref[0]
        n_scr[...] = n0_ref[0]
        m_scr[...] = m0_ref[0]

    L, G = chunk, chunks_per_step
    row = lax.broadcasted_iota(jnp.int32, (L, L), 0)
    col = lax.broadcasted_iota(jnp.int32, (L, L), 1)
    tril = col <= row
    triu = row <= col
    units = [(c, h) for c in range(G) for h in range(H_B)]
    rows_of = lambda c: slice(c * L, (c + 1) * L)
    qk_cols = lambda h: slice(h * DQK_B, (h + 1) * DQK_B)
    v_cols = lambda h: slice(h * DV_B, (h + 1) * DV_B)
    stack = lambda parts: jnp.stack(parts, axis=0)

    g_col = g_ref[0]
    g_row = gt_ref[0]
    lf_col = jax.nn.log_sigmoid(g_col)
    lf_row = jax.nn.log_sigmoid(g_row)
    ig_col = stack([g_col[rows_of(c), h:h + 1] for c, h in units])
    f_col = stack([lf_col[rows_of(c), H_B + h:H_B + h + 1] for c, h in units])
    ig_row = stack([g_row[h:h + 1, rows_of(c)] for c, h in units])
    f_row = stack([lf_row[H_B + h:H_B + h + 1, rows_of(c)] for c, h in units])
    b_col = jnp.sum(jnp.where(tril, f_row, 0.0), axis=2, keepdims=True)
    b_row = jnp.sum(jnp.where(triu, f_col, 0.0), axis=1, keepdims=True)
    g_tot = jnp.sum(f_row, axis=2, keepdims=True)
    dm = jnp.where(tril, b_col - b_row + ig_row, NEG)
    dm_max = jnp.max(dm, axis=2, keepdims=True)
    r_row = g_tot - b_row + ig_row
    r_col = g_tot - b_col + ig_col
    r_max = jnp.max(r_row, axis=2, keepdims=True)

    m_steps = [m_scr[...]]
    for c in range(G):
        heads = slice(c * H_B, (c + 1) * H_B)
        m_steps.append(jnp.maximum(g_tot[heads] + m_steps[c], r_max[heads]))
    m_cur = jnp.concatenate(m_steps[:G], axis=0)
    m_next = jnp.concatenate(m_steps[1:], axis=0)
    inter = b_col + m_cur
    m_t = jnp.maximum(inter, dm_max)
    a = jnp.exp(inter - m_t)
    floor = jnp.exp(-m_t)
    w_in = jnp.exp(dm - m_t)
    wr_row = jnp.exp(r_row - m_next)
    wr_col = jnp.exp(r_col - m_next)
    decay = jnp.exp(g_tot + m_cur - m_next)

    qk = stack([_dot_nt(q_ref[0, rows_of(c), qk_cols(h)], k_ref[0, rows_of(c), qk_cols(h)].astype(BF16))
                for c, h in units]) * w_in
    qk_sum = jnp.sum(qk, axis=2, keepdims=True)
    c_add, n_add, within = [], [], []
    for u, (c, h) in enumerate(units):
        vh = v_ref[0, rows_of(c), v_cols(h)]
        kw_t = kt_ref[0, qk_cols(h), rows_of(c)] * wr_row[u]
        c_add.append(_dot(kw_t.astype(BF16), vh))
        n_add.append(jnp.sum(k_ref[0, rows_of(c), qk_cols(h)] * wr_col[u], axis=0, keepdims=True))
        within.append(_dot(qk[u].astype(BF16), vh))

    c_cur = [c_scr[h] for h in range(H_B)]
    n_cur = [n_scr[h] for h in range(H_B)]
    for c in range(G):
        heads = slice(c * H_B, (c + 1) * H_B)
        q_heads = [q_ref[0, rows_of(c), qk_cols(h)] for h in range(H_B)]
        from_memory = [_dot(q_heads[h], c_cur[h].astype(BF16)) for h in range(H_B)]
        qn = jnp.sum(stack(q_heads).astype(F32) * stack(n_cur), axis=2, keepdims=True)
        den = a[heads] * qn + qk_sum[heads]
        div = jnp.maximum(jnp.abs(den), floor[heads])
        for h in range(H_B):
            u = c * H_B + h
            hs_ref[0, rows_of(c), v_cols(h)] = (a[u] * from_memory[h] + within[u]) / div[h]
            c_cur[h] = decay[u] * c_cur[h] + c_add[u]
            n_cur[h] = decay[u] * n_cur[h] + n_add[u]
    for h in range(H_B):
        c_scr[h] = c_cur[h]
        n_scr[h] = n_cur[h]
    m_scr[...] = m_steps[G]

    @pl.when(step == pl.num_programs(1) - 1)
    def _():
        ct_ref[0] = c_scr[...]
        nt_ref[0] = n_scr[...]
        mt_ref[0] = m_scr[...]


def _mlstm(q, k, kt, v, gates, gt, c0, n0, m0, batch, seq):
    chunk = CHUNK if seq % CHUNK == 0 else seq
    cps = max(1, min(MLSTM_SPAN // chunk, seq // chunk))
    span = chunk * cps
    nqk, nv = H_B * DQK_B, H_B * DV_B
    rows = lambda width: pl.BlockSpec((1, span, width), lambda b, s: (b, s, 0))
    cols = lambda height: pl.BlockSpec((1, height, span), lambda b, s: (b, 0, s))
    state = lambda *shape: pl.BlockSpec((1,) + shape, lambda b, s: (b,) + (0,) * len(shape))
    return pl.pallas_call(
        functools.partial(_mlstm_kernel, chunk=chunk, chunks_per_step=cps),
        grid=(batch, seq // span),
        in_specs=[rows(nqk), rows(nqk), cols(nqk), rows(nv), rows(LANES), cols(2 * H_B),
                  state(H_B, DQK_B, DV_B), state(H_B, 1, DQK_B), state(H_B, 1, 1)],
        out_specs=[rows(nv), state(H_B, DQK_B, DV_B), state(H_B, 1, DQK_B), state(H_B, 1, 1)],
        out_shape=[jax.ShapeDtypeStruct((batch, seq, nv), F32),
                   jax.ShapeDtypeStruct((batch, H_B, DQK_B, DV_B), F32),
                   jax.ShapeDtypeStruct((batch, H_B, 1, DQK_B), F32),
                   jax.ShapeDtypeStruct((batch, H_B, 1, 1), F32)],
        scratch_shapes=[pltpu.VMEM((H_B, DQK_B, DV_B), F32), pltpu.VMEM((H_B, 1, DQK_B), F32),
                        pltpu.VMEM((H_B, 1, 1), F32)],
        compiler_params=_params(("arbitrary", "arbitrary")),
        name="mlstm",
    )(q, k, kt, v, gates, gt, c0, n0, m0)


def _tail_kernel(*refs, gated_heads, final):
    if gated_heads:
        hs_ref, opre_ref, gout_ref = refs[:3]
        refs = refs[3:]
        hs, gout = hs_ref[...], gout_ref[...]
        normed = [_rms(hs[:, h * DV_B:(h + 1) * DV_B], gout[:, h * DV_B:(h + 1) * DV_B]) for h in range(H_B)]
        mixed = (jax.nn.sigmoid(opre_ref[...]) * jnp.concatenate(normed, axis=-1)).astype(BF16)
    else:
        mixed = refs[0][...]
        refs = refs[1:]
    (x_ref, gt1_ref, wo_ref, sh_ref, sc_ref, gt_ref, g_ref, win_ref, wout_ref, gfin_ref, y_ref, acc_scr) = refs
    x = x_ref[...] + gt1_ref[...] * _dot(mixed, wo_ref[...])
    h = (_rms(x, g_ref[...]) * (1.0 + sc_ref[...]) + sh_ref[...]).astype(BF16)
    for c in range(D_FF // FF_CHUNK):
        a = _dot(h, win_ref[:, c * FF_CHUNK:(c + 1) * FF_CHUNK])
        b = _dot(h, win_ref[:, D_FF + c * FF_CHUNK:D_FF + (c + 1) * FF_CHUNK])
        act = (a * jax.nn.sigmoid(a) * b).astype(BF16)
        part = _dot(act, wout_ref[c * FF_CHUNK:(c + 1) * FF_CHUNK, :])
        if c == 0:
            acc_scr[...] = part
        else:
            acc_scr[...] += part
    y = x + gt_ref[...] * acc_scr[...]
    y_ref[...] = _rms(y, gfin_ref[...]) if final else y


def _tail(path, mixer_out, x, mods, layer, w_o, j, g, w_in, w_out, g_final, final):
    gated_heads = len(mixer_out) == 3
    mixer_specs = [path.rows(D_MODEL)]
    if gated_heads:
        mixer_specs += [path.rows(D_MODEL), _const((1, D_MODEL))]
    mod = lambda part: path.mod((layer, part))
    return pl.pallas_call(
        functools.partial(_tail_kernel, gated_heads=gated_heads, final=final),
        grid=(path.steps,),
        in_specs=mixer_specs + [path.rows(D_MODEL), mod(2), _layer((D_MODEL, D_MODEL), j),
                                mod(3), mod(4), mod(5), _const((1, D_MODEL)),
                                _layer((D_MODEL, 2 * D_FF), layer), _layer((D_FF, D_MODEL), layer),
                                _const((1, D_MODEL))],
        out_specs=path.rows(D_MODEL),
        out_shape=jax.ShapeDtypeStruct((path.n, D_MODEL), F32),
        scratch_shapes=[pltpu.VMEM((path.tm, D_MODEL), F32)],
        compiler_params=_params(("arbitrary",)),
        name="tail_b" if gated_heads else "tail",
    )(*mixer_out, x, mods, w_o, mods, mods, mods, g, w_in, w_out, g_final)


def _prep_weights(p):
    w = {}
    w['a_qkv'] = p['w_a_qkv'].astype(BF16)
    w['a_o'] = p['w_a_o'].astype(BF16)
    nqk, nv = H_B * DQK_B, H_B * DV_B
    w_b = p['w_b_in']
    gate_pad = jnp.zeros(w_b.shape[:2] + (LANES - 2 * H_B,), w_b.dtype)
    w['b_in'] = jnp.concatenate([w_b, gate_pad], axis=-1).astype(BF16)
    bg = p['b_b_gates']
    w['b_gates'] = jnp.concatenate([bg, jnp.zeros((bg.shape[0], LANES - 2 * H_B), bg.dtype)], axis=-1)[:, None, :]
    w['b_out'] = p['w_b_out'].astype(BF16)
    n_c = p['w_c_dq'].shape[0]
    w_dkv = p['w_c_dkv']
    w['c_1'] = jnp.concatenate([p['w_c_dq'], w_dkv, jnp.zeros((n_c, D_MODEL, LANES - ROPE_C), w_dkv.dtype)],
                               axis=-1).astype(BF16)
    w_uq = p['w_c_uq'].reshape(n_c, Q_LORA, H_C, NOPE_C + ROPE_C)
    w_uq = jnp.concatenate([w_uq, jnp.zeros((n_c, Q_LORA, H_C, DQK_C_PAD - NOPE_C - ROPE_C), w_uq.dtype)], axis=-1)
    w['c_uq'] = w_uq.reshape(n_c, Q_LORA, H_C * DQK_C_PAD).astype(BF16)
    w['c_uq_t'] = jnp.swapaxes(w['c_uq'], 1, 2)
    w['c_ukv'] = p['w_c_ukv'].astype(BF16)
    w_ukv = w['c_ukv'].reshape(n_c, KV_LORA, H_C, NOPE_C + V_C)
    w['c_uk'] = w_ukv[..., :NOPE_C].reshape(n_c, KV_LORA, H_C * NOPE_C)
    w['c_uv_t'] = jnp.swapaxes(w_ukv[..., NOPE_C:].reshape(n_c, KV_LORA, H_C * V_C), 1, 2)
    w['c_o'] = p['w_c_o'].astype(BF16)
    w['ffn_in'] = p['w_ffn_in'].astype(BF16)
    w['ffn_out'] = p['w_ffn_out'].astype(BF16)
    return w


def _trunk(x, mods, past, p, w):
    batch, seq, _ = x.shape
    path = _Path(batch, seq)
    n_past = 0 if past is None else past[5].shape[2]
    pos = n_past + jnp.arange(seq, dtype=jnp.int32)
    tab_a = tuple(path.expand_table(t) for t in _rope_tables(pos, ROT_A, D_SUB))
    tab_c = tuple(path.expand_table(t) for t in _rope_tables(pos, ROPE_C, LANES))
    tab_c_t = _rope_tables_t(pos, ROPE_C)
    x = x.reshape(path.n, D_MODEL)
    g_final = p['g_final'][None, :]
    n_a = (DEPTH + N_MIXERS - 1) // N_MIXERS
    a_rows = None
    b_c, b_n, b_m, c_kv, c_kr = [], [], [], [], []
    mods = path.expand_mod(mods)
    for i in range(DEPTH):
        kind, j = i % N_MIXERS, i // N_MIXERS
        g1 = p['g_norm1'][i][None, :]
        if kind == 0:
            lambda_init = 0.8 - 0.6 * math.exp(-0.3 * i)
            q, k, v, k_row, v_row = _proj_a(path, x, mods, i, g1, w['a_qkv'], tab_a, j, n_a, a_rows)
            a_rows = (k_row, v_row)
            shape3 = (batch, seq, D_MODEL)
            if past is None:
                g_sub_t = jnp.broadcast_to(p['g_a_sub'][:, :, None], p['g_a_sub'].shape + (TB,))
                o = _flash(q, k.reshape(shape3), v, batch, seq, H_A, LANES,
                           diff_args=(p['a_lambda'], g_sub_t, j), lambda_init=lambda_init)
            else:
                o = _decode_a(q.reshape(shape3), k.reshape(shape3), v.reshape(shape3), past[0], past[1],
                              p['a_lambda'], p['g_a_sub'][:, None, :], j, lambda_init, batch, seq)
            mixer_out, w_o = (o.reshape(path.n, D_MODEL),), w['a_o']
        elif kind == 1:
            q, k, kt, v, opre, gates, gt = _proj_b(path, x, mods, i, g1, j, w['b_in'], w['b_gates'][j])
            nqk, nv = H_B * DQK_B, H_B * DV_B
            if path.table_rows != seq:
                kt = kt.reshape(nqk, batch, seq).transpose(1, 0, 2)
                gt = gt.reshape(2 * H_B, batch, seq).transpose(1, 0, 2)
            if past is None:
                c0 = jnp.zeros((batch, H_B, DQK_B, DV_B), F32)
                n0 = jnp.zeros((batch, H_B, 1, DQK_B), F32)
                m0 = jnp.zeros((batch, H_B, 1, 1), F32)
            else:
                c0 = past[2][j]
                n0 = past[3][j][:, :, None, :]
                m0 = past[4][j][:, :, None, None]
            hs, c_t, n_t, m_t = _mlstm(q.reshape(batch, seq, nqk), k.reshape(batch, seq, nqk), kt,
                                       v.reshape(batch, seq, nv), gates.reshape(batch, seq, LANES), gt,
                                       c0, n0, m0, batch, seq)
            b_c.append(c_t)
            b_n.append(n_t[:, :, 0, :])
            b_m.append(m_t[:, :, 0, 0])
            mixer_out, w_o = (hs.reshape(path.n, nv), opre, p['g_b_out'][j][None, :]), w['b_out']
        else:
            q, k, v, kv_row, kr_row = _proj_c(path, x, mods, i, g1, j, w, p['g_c_q'][j][None, :],
                                              p['g_c_kv'][j][None, :], tab_c, tab_c_t)
            qk_shape = (batch, seq, H_C * DQK_C_PAD)
            v_shape = (batch, seq, H_C * V_C)
            if past is None:
                o = _flash(q, k.reshape(qk_shape), v, batch, seq, H_C, DQK_C_PAD)
            else:
                o = _decode_c(q.reshape(qk_shape), k.reshape(qk_shape), v.reshape(v_shape), past[5], past[6],
                              w['c_ukv'], j, batch, seq)
            c_kv.append(kv_row.reshape(batch, seq, KV_LORA))
            c_kr.append(kr_row.reshape(batch, seq, ROPE_C))
            mixer_out, w_o = (o.reshape(path.n, D_MODEL),), w['c_o']
        x = _tail(path, mixer_out, x, mods, i, w_o, j, p['g_norm2'][i][None, :],
                  w['ffn_in'], w['ffn_out'], g_final, final=(i == DEPTH - 1))
    y = x.reshape(batch, seq, D_MODEL)
    a_k, a_v = (r.reshape(n_a, batch, seq, H_A, 2 * D_SUB) for r in a_rows)
    return y, (a_k, a_v, jnp.stack(b_c), jnp.stack(b_n), jnp.stack(b_m), jnp.stack(c_kv), jnp.stack(c_kr))


def kernel(x_prompt, x_sample, c_prompt, c_sample, cache_a_k, cache_a_v, state_b_c, state_b_n, state_b_m, cache_c_kv, cache_c_kr, w_ada, b_ada, g_norm1, g_norm2, w_a_qkv, a_lambda, g_a_sub, w_a_o, w_b_in, b_b_gates, g_b_out, w_b_out, w_c_dq, g_c_q, w_c_uq, w_c_dkv, g_c_kv, w_c_ukv, w_c_o, w_ffn_in, w_ffn_out, g_final):
    p = dict(g_norm1=g_norm1, g_norm2=g_norm2, w_a_qkv=w_a_qkv, a_lambda=a_lambda, g_a_sub=g_a_sub,
             w_a_o=w_a_o, w_b_in=w_b_in, b_b_gates=b_b_gates, g_b_out=g_b_out, w_b_out=w_b_out,
             w_c_dq=w_c_dq, g_c_q=g_c_q, w_c_uq=w_c_uq, w_c_dkv=w_c_dkv, g_c_kv=g_c_kv, w_c_ukv=w_c_ukv,
             w_c_o=w_c_o, w_ffn_in=w_ffn_in, w_ffn_out=w_ffn_out, g_final=g_final)
    w = _prep_weights(p)
    nb_p = x_prompt.shape[0]
    mods = _ada(jnp.concatenate([c_prompt, c_sample], axis=0), w_ada, b_ada)
    y_prompt, sp = _trunk(x_prompt, mods[:, :nb_p], None, p, w)
    past = (cache_a_k.reshape(-1, 2 * D_SUB), cache_a_v.reshape(-1, 2 * D_SUB),
            state_b_c, state_b_n, state_b_m, cache_c_kv, cache_c_kr)
    y_sample, ss = _trunk(x_sample, mods[:, nb_p:], past, p, w)
    return (y_prompt, y_sample) + sp + ss
```

```python
import functools
import math

import jax
import jax.numpy as jnp
from jax import lax
from jax.experimental import pallas as pl
from jax.experimental.pallas import tpu as pltpu

F32 = jnp.float32
BF16 = jnp.bfloat16

D_MODEL = 1024
DEPTH = 4
CHUNK = 64
N_MIXERS = 3
ROPE_THETA = 500000.0
EPS = 1e-6
D_SUB = 64
H_A = D_MODEL // (2 * D_SUB)
ROT_A = D_SUB // 4
H_B = 4
DQK_B = D_MODEL // (2 * H_B)
DV_B = D_MODEL // H_B
H_C = 8
NOPE_C = 128
ROPE_C = 64
V_C = 128
Q_LORA = 384
KV_LORA = 256
D_FF = -(-8 * D_MODEL // (3 * 256)) * 256

LANES = 128
BF16_ROWS = 16
DQK_C_PAD = 2 * LANES
NEG = -1e30
VMEM_LIMIT = 56 * 1024 * 1024
TM = 512
TB = 256
HEAD_GROUP = 4
FF_CHUNK = 256
MLSTM_SPAN = 512
LOG2E = math.log2(math.e)
QSCALE_A = D_SUB ** -0.5 * LOG2E
QSCALE_C = (NOPE_C + ROPE_C) ** -0.5 * LOG2E


def _params(sem):
    return pltpu.CompilerParams(dimension_semantics=sem, vmem_limit_bytes=VMEM_LIMIT)


def _const(shape):
    nd = len(shape)
    return pl.BlockSpec(shape, lambda *_: (0,) * nd, pipeline_mode=pl.Buffered(1))


def _layer(shape, j):
    nd = len(shape)
    return pl.BlockSpec((None,) + tuple(shape), lambda *_: (j,) + (0,) * nd, pipeline_mode=pl.Buffered(1))


def _rms(x, g):
    return x * lax.rsqrt(jnp.mean(x * x, axis=-1, keepdims=True) + EPS) * g


def _dot(a, b):
    return jnp.dot(a, b, preferred_element_type=F32)


def _dot_nt(a, b):
    return lax.dot_general(a, b, (((1,), (1,)), ((), ())), preferred_element_type=F32)


def _rope_cols(c, cos, sin_lo, sin_hi, half):
    return c * cos + pltpu.roll(c, LANES - half, 1) * sin_lo + pltpu.roll(c, half, 1) * sin_hi


def _ada_kernel(c_ref, w_ref, b_ref, o_ref):
    c = c_ref[...]
    s = (c * jax.nn.sigmoid(c)).astype(BF16)
    o_ref[0] = _dot(s, w_ref[0].astype(BF16)) + b_ref[0]


def _ada(c_all, w_ada, b_ada):
    n = c_all.shape[0]
    tn = 1536
    return pl.pallas_call(
        _ada_kernel,
        grid=(DEPTH, 6 * D_MODEL // tn),
        in_specs=[pl.BlockSpec((n, D_MODEL), lambda i, j: (0, 0)),
                  pl.BlockSpec((1, D_MODEL, tn), lambda i, j: (i, 0, j)),
                  pl.BlockSpec((1, 1, tn), lambda i, j: (i, 0, j))],
        out_specs=pl.BlockSpec((1, n, tn), lambda i, j: (i, 0, j)),
        out_shape=jax.ShapeDtypeStruct((DEPTH, n, 6 * D_MODEL), F32),
        compiler_params=_params(("arbitrary", "arbitrary")),
        name="ada",
    )(c_all, w_ada, b_ada.reshape(DEPTH, 1, 6 * D_MODEL))


class _Path:
    def __init__(self, batch, seq):
        self.batch, self.seq = batch, seq
        self.n = batch * seq
        if seq >= TM:
            self.tm, self.tpb, self.mod_rows = TM, seq // TM, 1
        else:
            self.tm, self.tpb, self.mod_rows = self.n, 1, self.n
        self.steps = self.n // self.tm
        self.table_rows = self.tpb * self.tm

    def rows(self, width):
        return pl.BlockSpec((self.tm, width), lambda i: (i, 0))

    def mod(self, which):
        tpb = self.tpb
        layer, part = which
        return pl.BlockSpec((None, None, self.mod_rows, D_MODEL), lambda i: (layer, i // tpb, 0, part))

    def table(self):
        tpb = self.tpb
        return pl.BlockSpec((self.tm, LANES), lambda i: (i % tpb, 0))

    def expand_mod(self, m):
        if self.mod_rows == 1:
            return m[:, :, None, :]
        return jnp.repeat(m, self.seq, axis=1)[:, None]

    def expand_table(self, t):
        if self.table_rows == self.seq:
            return t
        return jnp.tile(t, (self.table_rows // self.seq, 1))


def _rope_tables(pos, rot, period):
    half = rot // 2
    inv = jnp.power(jnp.float32(ROPE_THETA), -jnp.arange(half, dtype=jnp.float32) * (2.0 / rot))
    ang = pos.astype(jnp.float32)[:, None] * inv[None, :]
    cos, sin = jnp.cos(ang), jnp.sin(ang)
    n = pos.shape[0]
    reps = LANES // period
    pad1 = jnp.ones((n, period - rot), F32)
    pad0 = jnp.zeros((n, period - rot), F32)
    zh = jnp.zeros((n, half), F32)
    cos_t = jnp.tile(jnp.concatenate([cos, cos, pad1], axis=1), (1, reps))
    sin_lo = jnp.tile(jnp.concatenate([-sin, zh, pad0], axis=1), (1, reps))
    sin_hi = jnp.tile(jnp.concatenate([zh, sin, pad0], axis=1), (1, reps))
    return cos_t, sin_lo, sin_hi


def _rope_tables_t(pos, rot):
    half = rot // 2
    inv = jnp.power(jnp.float32(ROPE_THETA), -jnp.arange(half, dtype=jnp.float32) * (2.0 / rot))
    ang = pos.astype(jnp.float32)[:, None] * inv[None, :]
    return jnp.cos(ang).T, jnp.sin(ang).T


def _proj_a_kernel(*refs, transposed, aliased, tm, slot):
    refs = refs[2:] if aliased else refs
    (x_ref, sh_ref, sc_ref, g_ref, w_ref, cos_ref, slo_ref, shi_ref,
     q_ref, k_ref, v_ref, krow_all, vrow_all) = refs
    if aliased:
        krow_ref, vrow_ref = krow_all.at[0], vrow_all.at[0]
    else:
        krow_ref, vrow_ref = krow_all.at[slot], vrow_all.at[slot]
        for other in range(krow_all.shape[0]):
            if other != slot:
                krow_all[other] = jnp.zeros(krow_all.shape[1:], F32)
                vrow_all[other] = jnp.zeros(vrow_all.shape[1:], F32)
    h = (_rms(x_ref[...], g_ref[...]) * (1.0 + sc_ref[...]) + sh_ref[...]).astype(BF16)
    cos, slo, shi = cos_ref[...], slo_ref[...], shi_ref[...]
    q = _dot(h, w_ref[:, 0:D_MODEL])
    k = _dot(h, w_ref[:, D_MODEL:2 * D_MODEL])
    v = _dot(h, w_ref[:, 2 * D_MODEL:3 * D_MODEL])
    for j in range(H_A):
        cs = slice(j * LANES, (j + 1) * LANES)
        qj = _rope_cols(q[:, cs], cos, slo, shi, ROT_A // 2) * QSCALE_A
        kj = _rope_cols(k[:, cs], cos, slo, shi, ROT_A // 2)
        vj = v[:, cs]
        krow_ref[pl.ds(j, tm, stride=H_A), :] = kj
        vrow_ref[pl.ds(j, tm, stride=H_A), :] = vj
        k_ref[:, cs] = kj.astype(BF16)
        if transposed:
            qt, vt = qj.T.astype(BF16), vj.T.astype(BF16)
            for t in range(tm // TB):
                q_ref[0, t, cs, :] = qt[:, t * TB:(t + 1) * TB]
                v_ref[0, t, cs, :] = vt[:, t * TB:(t + 1) * TB]
        else:
            q_ref[:, cs] = qj.astype(BF16)
            v_ref[:, cs] = vj.astype(BF16)


def _proj_a(path, x, mods, layer, g, w_qkv, tables, layer_slot, n_slots, prev_rows):
    n, tm, tpb, steps = path.n, path.tm, path.tpb, path.steps
    transposed = path.seq >= TM
    aliased = prev_rows is not None
    bf = jax.ShapeDtypeStruct((n, D_MODEL), BF16)
    if transposed:
        bft = jax.ShapeDtypeStruct((path.batch, path.seq // TB, D_MODEL, TB), BF16)
        tspec = pl.BlockSpec((1, tm // TB, D_MODEL, TB), lambda i: (i // tpb, i % tpb, 0, 0))
        qv_shape, qv_spec = bft, tspec
    else:
        qv_shape, qv_spec = bf, path.rows(D_MODEL)
    rows_shape = jax.ShapeDtypeStruct((n_slots, n * H_A, LANES), F32)
    if aliased:
        rows_spec = pl.BlockSpec((1, tm * H_A, LANES), lambda i: (layer_slot, i, 0))
    else:
        rows_spec = pl.BlockSpec((n_slots, tm * H_A, LANES), lambda i: (0, i, 0))
    in_specs = [path.rows(D_MODEL), path.mod((layer, 0)), path.mod((layer, 1)), _const((1, D_MODEL)),
                _layer((D_MODEL, 3 * D_MODEL), layer_slot), path.table(), path.table(), path.table()]
    args = [x, mods, mods, g, w_qkv, *tables]
    aliases = {}
    if aliased:
        in_specs = [pl.BlockSpec(memory_space=pl.ANY)] * 2 + in_specs
        args = list(prev_rows) + args
        aliases = {0: 3, 1: 4}
    return pl.pallas_call(
        functools.partial(_proj_a_kernel, transposed=transposed, aliased=aliased, tm=tm, slot=layer_slot),
        grid=(steps,),
        in_specs=in_specs,
        out_specs=[qv_spec, path.rows(D_MODEL), qv_spec, rows_spec, rows_spec],
        out_shape=[qv_shape, bf, qv_shape, rows_shape, rows_shape],
        input_output_aliases=aliases,
        compiler_params=_params(("arbitrary",)),
        name="proj_a",
    )(*args)


def _lambda_full(lam_ref, lambda_init):
    lv = lam_ref[0]
    d1 = jnp.sum(lv[0:1] * lv[1:2], axis=-1, keepdims=True)
    d2 = jnp.sum(lv[2:3] * lv[3:4], axis=-1, keepdims=True)
    return jnp.exp(d1) - jnp.exp(d2) + lambda_init


def _split_subheads(q):
    lane = lax.broadcasted_iota(jnp.int32, q.shape, 1)
    zero = jnp.zeros_like(q)
    return [jnp.where(lane < D_SUB, q, zero), jnp.where(lane >= D_SUB, q, zero)]


def _diff_combine(o1, o2, lam, g_sub, lambda_init):
    o = o1 - lam * o2
    return _rms(o, g_sub) * (1.0 - lambda_init)


def _flash_kernel(*refs, diff, lambda_init, n_tiles, heads, dqk):
    if diff:
        qt_ref, k_ref, vt_ref, lam_ref, gsub_ref, o_ref, m_scr, acc_scr, sa_scr, sb_scr = refs
    else:
        qt_ref, k_ref, vt_ref, o_ref, m_scr, acc_scr, sa_scr, sb_scr = refs
    n_sub = 2 if diff else 1
    key = lax.broadcasted_iota(jnp.int32, (TB, TB), 0)
    qry = lax.broadcasted_iota(jnp.int32, (TB, TB), 1)
    visible = (key // CHUNK) <= (qry // CHUNK)
    ones_rows = jnp.ones((BF16_ROWS, TB), BF16)

    def normalised(c):
        acc = acc_scr[c]
        return acc[0:LANES] / acc[LANES:LANES + 1]

    def chain_queries(i):
        qs = []
        for h in range(heads):
            qt = qt_ref[0, i, h * dqk:(h + 1) * dqk, :]
            if diff:
                dim = lax.broadcasted_iota(jnp.int32, qt.shape, 0)
                zero = jnp.zeros_like(qt)
                qs += [jnp.where(dim < D_SUB, qt, zero), jnp.where(dim >= D_SUB, qt, zero)]
            else:
                qs.append(qt)
        return qs

    def score_block(qs, j, s_scr):
        rows = pl.ds(pl.multiple_of(j * TB, TB), TB)
        for h in range(heads):
            kb = k_ref[0, rows, h * dqk:(h + 1) * dqk]
            for s in range(n_sub):
                s_scr[h * n_sub + s] = _dot(kb, qs[h * n_sub + s])

    score_block(chain_queries(0), 0, sa_scr)

    def q_tile(i, carry):
        qs = chain_queries(i)
        scores = functools.partial(score_block, qs)
        m_scr[...] = jnp.full(m_scr.shape, NEG, F32)
        acc_scr[...] = jnp.zeros(acc_scr.shape, F32)

        def update(j, s_scr, masked):
            for h in range(heads):
                vb = jnp.concatenate([vt_ref[0, j, h * LANES:(h + 1) * LANES, :], ones_rows], axis=0)
                for s in range(n_sub):
                    c = h * n_sub + s
                    st = jnp.where(visible, s_scr[c], NEG) if masked else s_scr[c]
                    m_prev = m_scr[c]
                    m_new = jnp.maximum(m_prev, jnp.max(st, axis=0, keepdims=True))
                    p = jnp.exp2(st - m_new)
                    acc_scr[c] = jnp.exp2(m_prev - m_new) * acc_scr[c] + _dot(vb, p.astype(BF16))
                    m_scr[c] = m_new

        def pair(t, c):
            scores(2 * t + 1, sb_scr)
            update(2 * t, sa_scr, False)
            scores(2 * t + 2, sa_scr)
            update(2 * t + 1, sb_scr, False)
            return c

        lax.fori_loop(0, i // 2, pair, 0)

        @pl.when(i % 2 == 0)
        def _():
            update(i, sa_scr, True)

        @pl.when(i % 2 == 1)
        def _():
            scores(i, sb_scr)
            update(i - 1, sa_scr, False)
            update(i, sb_scr, True)

        score_block(chain_queries(jnp.minimum(i + 1, n_tiles - 1)), 0, sa_scr)
        out_rows = pl.ds(pl.multiple_of(i * TB, TB), TB)
        for h in range(heads):
            if diff:
                ot = normalised(2 * h) - _lambda_full(lam_ref, lambda_init) * normalised(2 * h + 1)
                ot = (ot * lax.rsqrt(jnp.mean(ot * ot, axis=0, keepdims=True) + EPS) * gsub_ref[0]
                      * (1.0 - lambda_init))
            else:
                ot = normalised(h)
            o_ref[0, out_rows, h * LANES:(h + 1) * LANES] = ot.T.astype(BF16)
        return carry

    lax.fori_loop(0, n_tiles, q_tile, 0)


def _flash(qt, k, vt, batch, seq, heads, dqk, diff_args=None, lambda_init=0.0):
    diff = diff_args is not None
    hg = HEAD_GROUP
    n_tiles = seq // TB
    chains = hg * (2 if diff else 1)
    in_specs = [pl.BlockSpec((1, n_tiles, hg * dqk, TB), lambda b, g: (b, 0, g, 0)),
                pl.BlockSpec((1, seq, hg * dqk), lambda b, g: (b, 0, g)),
                pl.BlockSpec((1, n_tiles, hg * LANES, TB), lambda b, g: (b, 0, g, 0))]
    args = [qt, k, vt]
    if diff:
        lam, g_sub_t, jl = diff_args
        in_specs += [pl.BlockSpec((1, 4, D_SUB), lambda b, g: (jl, 0, 0)),
                     pl.BlockSpec((1, LANES, TB), lambda b, g: (jl, 0, 0))]
        args += [lam, g_sub_t]
    return pl.pallas_call(
        functools.partial(_flash_kernel, diff=diff, lambda_init=lambda_init, n_tiles=n_tiles, heads=hg, dqk=dqk),
        grid=(batch, heads // hg),
        in_specs=in_specs,
        out_specs=pl.BlockSpec((1, seq, hg * LANES), lambda b, g: (b, 0, g)),
        out_shape=jax.ShapeDtypeStruct((batch, seq, heads * LANES), BF16),
        scratch_shapes=[pltpu.VMEM((chains, 1, TB), F32),
                        pltpu.VMEM((chains, LANES + BF16_ROWS, TB), F32),
                        pltpu.VMEM((chains, TB, TB), F32), pltpu.VMEM((chains, TB, TB), F32)],
        compiler_params=_params(("arbitrary", "arbitrary")),
        name="flash_diff" if diff else "flash_mla",
    )(*args)


def _softmax_two(s_past, s_new):
    m = jnp.maximum(jnp.max(s_past, axis=-1, keepdims=True), jnp.max(s_new, axis=-1, keepdims=True))
    p_past = jnp.exp2(s_past - m)
    p_new = jnp.exp2(s_new - m)
    l = jnp.sum(p_past, axis=-1, keepdims=True) + jnp.sum(p_new, axis=-1, keepdims=True)
    return p_past, p_new, l


def _decode_a_kernel(q_ref, kn_ref, vn_ref, kp_ref, vp_ref, lam_ref, gsub_ref, o_ref, *, lambda_init, past, seq):
    lam = _lambda_full(lam_ref, lambda_init)
    q_all, kn_all, vn_all = q_ref[0], kn_ref[0], vn_ref[0]
    s_past, s_new = [], []
    for h in range(H_A):
        cs = slice(h * LANES, (h + 1) * LANES)
        q2 = jnp.concatenate(_split_subheads(q_all[:, cs]), axis=0)
        kp = kp_ref[pl.ds(h, past, stride=H_A), :].astype(BF16)
        s_past.append(_dot_nt(q2, kp))
        s_new.append(_dot_nt(q2, kn_all[:, cs]))
    for h in range(H_A):
        cs = slice(h * LANES, (h + 1) * LANES)
        vp = vp_ref[pl.ds(h, past, stride=H_A), :].astype(BF16)
        p_past, p_new, l = _softmax_two(s_past[h], s_new[h])
        o2 = (_dot(p_past.astype(BF16), vp) + _dot(p_new.astype(BF16), vn_all[:, cs])) / l
        o = _diff_combine(o2[0:seq], o2[seq:2 * seq], lam, gsub_ref[0], lambda_init)
        o_ref[0, :, cs] = o.astype(BF16)


def _decode_a(q, k, v, cache_k, cache_v, lam, g_sub, jl, lambda_init, batch, seq):
    past = cache_k.shape[0] // (H_A * batch * lam.shape[0])
    new = pl.BlockSpec((1, seq, D_MODEL), lambda b: (b, 0, 0))
    old = pl.BlockSpec((past * H_A, LANES), lambda b: (jl * batch + b, 0))
    return pl.pallas_call(
        functools.partial(_decode_a_kernel, lambda_init=lambda_init, past=past, seq=seq),
        grid=(batch,),
        in_specs=[new, new, new, old, old,
                  pl.BlockSpec((1, 4, D_SUB), lambda b: (jl, 0, 0)),
                  pl.BlockSpec((1, 1, LANES), lambda b: (jl, 0, 0))],
        out_specs=new,
        out_shape=jax.ShapeDtypeStruct((batch, seq, D_MODEL), BF16),
        compiler_params=_params(("arbitrary",)),
        name="decode_a",
    )(q, k, v, cache_k, cache_v, lam, g_sub)


def _proj_c_kernel(x_ref, sh_ref, sc_ref, g_ref, w1_ref, gq_ref, wuq_ref, gkv_ref, wukv_ref,
                   cos_ref, slo_ref, shi_ref, q_ref, k_ref, v_ref, kvrow_ref, krrow_ref):
    h = (_rms(x_ref[...], g_ref[...]) * (1.0 + sc_ref[...]) + sh_ref[...]).astype(BF16)
    cos, slo, shi = cos_ref[...], slo_ref[...], shi_ref[...]
    d = _dot(h, w1_ref[...])
    qlat = _rms(d[:, 0:Q_LORA], gq_ref[...]).astype(BF16)
    kv_row = _rms(d[:, Q_LORA:Q_LORA + KV_LORA], gkv_ref[...])
    kr = _rope_cols(d[:, Q_LORA + KV_LORA:], cos, slo, shi, ROPE_C // 2)
    kvrow_ref[...] = kv_row
    krrow_ref[...] = kr[:, 0:ROPE_C]
    kr_bf = kr.astype(BF16)
    q = _dot(qlat, wuq_ref[...]) * QSCALE_C
    kv = _dot(kv_row.astype(BF16), wukv_ref[...])

    for hh in range(H_C):
        lo = slice(hh * DQK_C_PAD, hh * DQK_C_PAD + LANES)
        hi = slice(hh * DQK_C_PAD + LANES, (hh + 1) * DQK_C_PAD)
        q_ref[:, lo] = q[:, lo].astype(BF16)
        q_ref[:, hi] = _rope_cols(q[:, hi], cos, slo, shi, ROPE_C // 2).astype(BF16)
        k_ref[:, lo] = kv[:, lo].astype(BF16)
        k_ref[:, hi] = kr_bf
        v_ref[:, hh * LANES:(hh + 1) * LANES] = kv[:, hi].astype(BF16)


def _proj_c_t_kernel(x_ref, sh_ref, sc_ref, g_ref, w1_ref, gq_ref, wuqt_ref, gkv_ref, wuk_ref, wuvt_ref,
                     cos_ref, slo_ref, shi_ref, cost_ref, sint_ref,
                     q_ref, k_ref, v_ref, kvrow_ref, krrow_ref, *, tm):
    h = (_rms(x_ref[...], g_ref[...]) * (1.0 + sc_ref[...]) + sh_ref[...]).astype(BF16)
    d = _dot(h, w1_ref[...])
    qlat = _rms(d[:, 0:Q_LORA], gq_ref[...])
    kv_row = _rms(d[:, Q_LORA:Q_LORA + KV_LORA], gkv_ref[...])
    kr = _rope_cols(d[:, Q_LORA + KV_LORA:], cos_ref[...], slo_ref[...], shi_ref[...], ROPE_C // 2)
    kvrow_ref[...] = kv_row
    krrow_ref[...] = kr[:, 0:ROPE_C]
    kr_bf = kr.astype(BF16)
    q_t = _dot(wuqt_ref[...], qlat.T.astype(BF16)) * QSCALE_C
    v_t = _dot(wuvt_ref[...], kv_row.T.astype(BF16))
    k_nope = _dot(kv_row.astype(BF16), wuk_ref[...])
    cos_t, sin_t = cost_ref[...], sint_ref[...]
    half = ROPE_C // 2
    for hh in range(H_C):
        base = hh * DQK_C_PAD
        x1 = q_t[base + NOPE_C:base + NOPE_C + half]
        x2 = q_t[base + NOPE_C + half:base + NOPE_C + ROPE_C]
        q_h = jnp.concatenate([q_t[base:base + NOPE_C], x1 * cos_t - x2 * sin_t, x2 * cos_t + x1 * sin_t,
                               q_t[base + NOPE_C + ROPE_C:base + DQK_C_PAD]], axis=0).astype(BF16)
        v_h = v_t[hh * V_C:(hh + 1) * V_C].astype(BF16)
        for t in range(tm // TB):
            q_ref[0, t, base:base + DQK_C_PAD, :] = q_h[:, t * TB:(t + 1) * TB]
            v_ref[0, t, hh * V_C:(hh + 1) * V_C, :] = v_h[:, t * TB:(t + 1) * TB]
        k_ref[:, base:base + NOPE_C] = k_nope[:, hh * NOPE_C:(hh + 1) * NOPE_C].astype(BF16)
        k_ref[:, base + NOPE_C:base + DQK_C_PAD] = kr_bf


def _proj_c(path, x, mods, layer, g, j, w, g_q, g_kv, tables, tables_t):
    n, tm, tpb = path.n, path.tm, path.tpb
    w1n = w['c_1'].shape[2]
    transposed = path.seq >= TM
    common_in = [path.rows(D_MODEL), path.mod((layer, 0)), path.mod((layer, 1)), _const((1, D_MODEL)),
                 _layer((D_MODEL, w1n), j), _const((1, Q_LORA))]
    row_outs = [path.rows(KV_LORA), path.rows(ROPE_C)]
    row_shapes = [jax.ShapeDtypeStruct((n, KV_LORA), F32), jax.ShapeDtypeStruct((n, ROPE_C), F32)]
    k_shape = jax.ShapeDtypeStruct((n, H_C * DQK_C_PAD), BF16)
    if transposed:
        tiled = lambda width: (jax.ShapeDtypeStruct((path.batch, path.seq // TB, width, TB), BF16),
                               pl.BlockSpec((1, tm // TB, width, TB), lambda i: (i // tpb, i % tpb, 0, 0)))
        q_shape, q_spec = tiled(H_C * DQK_C_PAD)
        v_shape, v_spec = tiled(H_C * V_C)
        table_t = pl.BlockSpec((ROPE_C // 2, tm), lambda i: (0, i % tpb))
        return pl.pallas_call(
            functools.partial(_proj_c_t_kernel, tm=tm),
            grid=(path.steps,),
            in_specs=common_in + [_layer((H_C * DQK_C_PAD, Q_LORA), j), _const((1, KV_LORA)),
                                  _layer((KV_LORA, H_C * NOPE_C), j), _layer((H_C * V_C, KV_LORA), j),
                                  path.table(), path.table(), path.table(), table_t, table_t],
            out_specs=[q_spec, path.rows(H_C * DQK_C_PAD), v_spec] + row_outs,
            out_shape=[q_shape, k_shape, v_shape] + row_shapes,
            compiler_params=_params(("arbitrary",)),
            name="proj_c",
        )(x, mods, mods, g, w['c_1'], g_q, w['c_uq_t'], g_kv, w['c_uk'], w['c_uv_t'], *tables, *tables_t)
    w_uq, w_ukv = w['c_uq'], w['c_ukv']
    tiled = lambda width: (jax.ShapeDtypeStruct((n, width), BF16), path.rows(width))
    q_shape, q_spec = tiled(H_C * DQK_C_PAD)
    v_shape, v_spec = tiled(H_C * V_C)
    return pl.pallas_call(
        _proj_c_kernel,
        grid=(path.steps,),
        in_specs=common_in + [_layer((Q_LORA, H_C * DQK_C_PAD), j), _const((1, KV_LORA)),
                              _layer((KV_LORA, H_C * (NOPE_C + V_C)), j),
                              path.table(), path.table(), path.table()],
        out_specs=[q_spec, path.rows(H_C * DQK_C_PAD), v_spec] + row_outs,
        out_shape=[q_shape, k_shape, v_shape] + row_shapes,
        compiler_params=_params(("arbitrary",)),
        name="proj_c",
    )(x, mods, mods, g, w['c_1'], g_q, w_uq, g_kv, w_ukv, *tables)


def _decode_c_kernel(q_ref, kn_ref, vn_ref, kvp_ref, krp_ref, wukv_ref, o_ref, *, seq):
    q_all, kn_all, vn_all = q_ref[0], kn_ref[0], vn_ref[0]
    kv_past = kvp_ref[0, 0].astype(BF16)
    kr_past = krp_ref[0, 0].astype(BF16)
    q_lat, q_rope, s_new = [], [], []
    for h in range(H_C):
        qh = q_all[:, h * DQK_C_PAD:(h + 1) * DQK_C_PAD]
        w_uk = wukv_ref[:, h * (NOPE_C + V_C):h * (NOPE_C + V_C) + NOPE_C]
        q_lat.append(_dot_nt(qh[:, 0:NOPE_C], w_uk).astype(BF16))
        q_rope.append(qh[:, NOPE_C:NOPE_C + ROPE_C])
        s_new.append(_dot_nt(qh, kn_all[:, h * DQK_C_PAD:(h + 1) * DQK_C_PAD]))
    s_past = (_dot_nt(jnp.concatenate(q_lat, axis=0), kv_past)
              + _dot_nt(jnp.concatenate(q_rope, axis=0), kr_past))
    p_past, p_new, l = _softmax_two(s_past, jnp.concatenate(s_new, axis=0))
    lat = _dot(p_past.astype(BF16), kv_past)
    for h in range(H_C):
        rows = slice(h * seq, (h + 1) * seq)
        w_uv = wukv_ref[:, h * (NOPE_C + V_C) + NOPE_C:(h + 1) * (NOPE_C + V_C)]
        o = (_dot(lat[rows].astype(BF16), w_uv)
             + _dot(p_new[rows].astype(BF16), vn_all[:, h * V_C:(h + 1) * V_C])) / l[rows]
        o_ref[0, :, h * V_C:(h + 1) * V_C] = o.astype(BF16)


def _decode_c(q, k, v, cache_kv, cache_kr, w_ukv, jl, batch, seq):
    past = cache_kv.shape[2]
    return pl.pallas_call(
        functools.partial(_decode_c_kernel, seq=seq),
        grid=(batch,),
        in_specs=[pl.BlockSpec((1, seq, H_C * DQK_C_PAD), lambda b: (b, 0, 0)),
                  pl.BlockSpec((1, seq, H_C * DQK_C_PAD), lambda b: (b, 0, 0)),
                  pl.BlockSpec((1, seq, H_C * V_C), lambda b: (b, 0, 0)),
                  pl.BlockSpec((1, 1, past, KV_LORA), lambda b: (jl, b, 0, 0)),
                  pl.BlockSpec((1, 1, past, ROPE_C), lambda b: (jl, b, 0, 0)),
                  _layer((KV_LORA, H_C * (NOPE_C + V_C)), jl)],
        out_specs=pl.BlockSpec((1, seq, H_C * V_C), lambda b: (b, 0, 0)),
        out_shape=jax.ShapeDtypeStruct((batch, seq, H_C * V_C), BF16),
        compiler_params=_params(("arbitrary",)),
        name="decode_c",
    )(q, k, v, cache_kv, cache_kr, w_ukv)


def _proj_b_kernel(x_ref, sh_ref, sc_ref, g_ref, w_ref, bg_ref,
                   q_ref, k_ref, kt_ref, v_ref, opre_ref, gates_ref, gt_ref):
    nqk, nv = H_B * DQK_B, H_B * DV_B
    h = (_rms(x_ref[...], g_ref[...]) * (1.0 + sc_ref[...]) + sh_ref[...]).astype(BF16)
    q_ref[...] = _dot(h, w_ref[:, 0:nqk]).astype(BF16)
    k = _dot(h, w_ref[:, nqk:2 * nqk]) * (DQK_B ** -0.5)
    k_ref[...] = k
    kt_ref[0] = k.T
    v_ref[...] = _dot(h, w_ref[:, 2 * nqk:2 * nqk + nv]).astype(BF16)
    opre_ref[...] = _dot(h, w_ref[:, 2 * nqk + nv:2 * nqk + 2 * nv])
    gates = _dot(h, w_ref[:, 2 * nqk + 2 * nv:]) + bg_ref[...]
    gates_ref[...] = gates
    gt_ref[0] = gates.T[0:2 * H_B, :]


def _proj_b(path, x, mods, layer, g, j, w_in, b_gates):
    n, nqk, nv = path.n, H_B * DQK_B, H_B * DV_B
    groups = path.n // path.table_rows
    tpb = path.tpb
    cols = lambda width: pl.BlockSpec((1, width, path.tm), lambda i: (i // tpb, 0, i % tpb))
    return pl.pallas_call(
        _proj_b_kernel,
        grid=(path.steps,),
        in_specs=[path.rows(D_MODEL), path.mod((layer, 0)), path.mod((layer, 1)), _const((1, D_MODEL)),
                  _layer((D_MODEL, w_in.shape[2]), j), _const((1, LANES))],
        out_specs=[path.rows(nqk), path.rows(nqk), cols(nqk), path.rows(nv), path.rows(nv),
                   path.rows(LANES), cols(2 * H_B)],
        out_shape=[jax.ShapeDtypeStruct((n, nqk), BF16), jax.ShapeDtypeStruct((n, nqk), F32),
                   jax.ShapeDtypeStruct((groups, nqk, path.table_rows), F32),
                   jax.ShapeDtypeStruct((n, nv), BF16), jax.ShapeDtypeStruct((n, nv), F32),
                   jax.ShapeDtypeStruct((n, LANES), F32),
                   jax.ShapeDtypeStruct((groups, 2 * H_B, path.table_rows), F32)],
        compiler_params=_params(("arbitrary",)),
        name="proj_b",
    )(x, mods, mods, g, w_in, b_gates)


def _mlstm_kernel(q_ref, k_ref, kt_ref, v_ref, g_ref, gt_ref, c0_ref, n0_ref, m0_ref,
                  hs_ref, ct_ref, nt_ref, mt_ref, c_scr, n_scr, m_scr, *, chunk, chunks_per_step):
    step = pl.program_id(1)

    @pl.when(step == 0)
    def _():
        c_scr[...] = c0_ref[0]
        n_scr[...] = n0_ref[0]
        m_scr[...] = m0_ref[0]

    L, G = chunk, chunks_per_step
    row = lax.broadcasted_iota(jnp.int32, (L, L), 0)
    col = lax.broadcasted_iota(jnp.int32, (L, L), 1)
    tril = col <= row
    triu = row <= col
    units = [(c, h) for c in range(G) for h in range(H_B)]
    rows_of = lambda c: slice(c * L, (c + 1) * L)
    qk_cols = lambda h: slice(h * DQK_B, (h + 1) * DQK_B)
    v_cols = lambda h: slice(h * DV_B, (h + 1) * DV_B)
    stack = lambda parts: jnp.stack(parts, axis=0)

    g_col = g_ref[0]
    g_row = gt_ref[0]
    lf_col = jax.nn.log_sigmoid(g_col)
    lf_row = jax.nn.log_sigmoid(g_row)
    ig_col = stack([g_col[rows_of(c), h:h + 1] for c, h in units])
    f_col = stack([lf_col[rows_of(c), H_B + h:H_B + h + 1] for c, h in units])
    ig_row = stack([g_row[h:h + 1, rows_of(c)] for c, h in units])
    f_row = stack([lf_row[H_B + h:H_B + h + 1, rows_of(c)] for c, h in units])
    b_col = jnp.sum(jnp.where(tril, f_row, 0.0), axis=2, keepdims=True)
    b_row = jnp.sum(jnp.where(triu, f_col, 0.0), axis=1, keepdims=True)
    g_tot = jnp.sum(f_row, axis=2, keepdims=True)
    dm = jnp.where(tril, b_col - b_row + ig_row, NEG)
    dm_max = jnp.max(dm, axis=2, keepdims=True)
    r_row = g_tot - b_row + ig_row
    r_col = g_tot - b_col + ig_col
    r_max = jnp.max(r_row, axis=2, keepdims=True)

    m_steps = [m_scr[...]]
    for c in range(G):
        heads = slice(c * H_B, (c + 1) * H_B)
        m_steps.append(jnp.maximum(g_tot[heads] + m_steps[c], r_max[heads]))
    m_cur = jnp.concatenate(m_steps[:G], axis=0)
    m_next = jnp.concatenate(m_steps[1:], axis=0)
    inter = b_col + m_cur
    m_t = jnp.maximum(inter, dm_max)
    a = jnp.exp(inter - m_t)
    floor = jnp.exp(-m_t)
    w_in = jnp.exp(dm - m_t)
    wr_row = jnp.exp(r_row - m_next)
    wr_col = jnp.exp(r_col - m_next)
    decay = jnp.exp(g_tot + m_cur - m_next)

    qk = stack([_dot_nt(q_ref[0, rows_of(c), qk_cols(h)], k_ref[0, rows_of(c), qk_cols(h)].astype(BF16))
                for c, h in units]) * w_in
    qk_sum = jnp.sum(qk, axis=2, keepdims=True)
    c_add, n_add, within = [], [], []
    for u, (c, h) in enumerate(units):
        vh = v_ref[0, rows_of(c), v_cols(h)]
        kw_t = kt_ref[0, qk_cols(h), rows_of(c)] * wr_row[u]
        c_add.append(_dot(kw_t.astype(BF16), vh))
        n_add.append(jnp.sum(k_ref[0, rows_of(c), qk_cols(h)] * wr_col[u], axis=0, keepdims=True))
        within.append(_dot(qk[u].astype(BF16), vh))

    c_cur = [c_scr[h] for h in range(H_B)]
    n_cur = [n_scr[h] for h in range(H_B)]
    for c in range(G):
        heads = slice(c * H_B, (c + 1) * H_B)
        q_heads = [q_ref[0, rows_of(c), qk_cols(h)] for h in range(H_B)]
        from_memory = [_dot(q_heads[h], c_cur[h].astype(BF16)) for h in range(H_B)]
        qn = jnp.sum(stack(q_heads).astype(F32) * stack(n_cur), axis=2, keepdims=True)
        den = a[heads] * qn + qk_sum[heads]
        div = jnp.maximum(jnp.abs(den), floor[heads])
        for h in range(H_B):
            u = c * H_B + h
            hs_ref[0, rows_of(c), v_cols(h)] = (a[u] * from_memory[h] + within[u]) / div[h]
            c_cur[h] = decay[u] * c_cur[h] + c_add[u]
            n_cur[h] = decay[u] * n_cur[h] + n_add[u]
    for h in range(H_B):
        c_scr[h] = c_cur[h]
        n_scr[h] = n_cur[h]
    m_scr[...] = m_steps[G]

    @pl.when(step == pl.num_programs(1) - 1)
    def _():
        ct_ref[0] = c_scr[...]
        nt_ref[0] = n_scr[...]
        mt_ref[0] = m_scr[...]


def _mlstm(q, k, kt, v, gates, gt, c0, n0, m0, batch, seq):
    chunk = CHUNK if seq % CHUNK == 0 else seq
    cps = max(1, min(MLSTM_SPAN // chunk, seq // chunk))
    span = chunk * cps
    nqk, nv = H_B * DQK_B, H_B * DV_B
    rows = lambda width: pl.BlockSpec((1, span, width), lambda b, s: (b, s, 0))
    cols = lambda height: pl.BlockSpec((1, height, span), lambda b, s: (b, 0, s))
    state = lambda *shape: pl.BlockSpec((1,) + shape, lambda b, s: (b,) + (0,) * len(shape))
    return pl.pallas_call(
        functools.partial(_mlstm_kernel, chunk=chunk, chunks_per_step=cps),
        grid=(batch, seq // span),
        in_specs=[rows(nqk), rows(nqk), cols(nqk), rows(nv), rows(LANES), cols(2 * H_B),
                  state(H_B, DQK_B, DV_B), state(H_B, 1, DQK_B), state(H_B, 1, 1)],
        out_specs=[rows(nv), state(H_B, DQK_B, DV_B), state(H_B, 1, DQK_B), state(H_B, 1, 1)],
        out_shape=[jax.ShapeDtypeStruct((batch, seq, nv), F32),
                   jax.ShapeDtypeStruct((batch, H_B, DQK_B, DV_B), F32),
                   jax.ShapeDtypeStruct((batch, H_B, 1, DQK_B), F32),
                   jax.ShapeDtypeStruct((batch, H_B, 1, 1), F32)],
        scratch_shapes=[pltpu.VMEM((H_B, DQK_B, DV_B), F32), pltpu.VMEM((H_B, 1, DQK_B), F32),
                        pltpu.VMEM((H_B, 1, 1), F32)],
        compiler_params=_params(("arbitrary", "arbitrary")),
        name="mlstm",
    )(q, k, kt, v, gates, gt, c0, n0, m0)


def _tail_kernel(*refs, gated_heads, final):
    if gated_heads:
        hs_ref, opre_ref, gout_ref = refs[:3]
        refs = refs[3:]
        hs, gout = hs_ref[...], gout_ref[...]
        normed = [_rms(hs[:, h * DV_B:(h + 1) * DV_B], gout[:, h * DV_B:(h + 1) * DV_B]) for h in range(H_B)]
        mixed = (jax.nn.sigmoid(opre_ref[...]) * jnp.concatenate(normed, axis=-1)).astype(BF16)
    else:
        mixed = refs[0][...]
        refs = refs[1:]
    (x_ref, gt1_ref, wo_ref, sh_ref, sc_ref, gt_ref, g_ref, win_ref, wout_ref, gfin_ref, y_ref, acc_scr) = refs
    x = x_ref[...] + gt1_ref[...] * _dot(mixed, wo_ref[...])
    h = (_rms(x, g_ref[...]) * (1.0 + sc_ref[...]) + sh_ref[...]).astype(BF16)
    for c in range(D_FF // FF_CHUNK):
        a = _dot(h, win_ref[:, c * FF_CHUNK:(c + 1) * FF_CHUNK])
        b = _dot(h, win_ref[:, D_FF + c * FF_CHUNK:D_FF + (c + 1) * FF_CHUNK])
        act = (a * jax.nn.sigmoid(a) * b).astype(BF16)
        part = _dot(act, wout_ref[c * FF_CHUNK:(c + 1) * FF_CHUNK, :])
        if c == 0:
            acc_scr[...] = part
        else:
            acc_scr[...] += part
    y = x + gt_ref[...] * acc_scr[...]
    y_ref[...] = _rms(y, gfin_ref[...]) if final else y


def _tail(path, mixer_out, x, mods, layer, w_o, j, g, w_in, w_out, g_final, final):
    gated_heads = len(mixer_out) == 3
    mixer_specs = [path.rows(D_MODEL)]
    if gated_heads:
        mixer_specs += [path.rows(D_MODEL), _const((1, D_MODEL))]
    mod = lambda part: path.mod((layer, part))
    return pl.pallas_call(
        functools.partial(_tail_kernel, gated_heads=gated_heads, final=final),
        grid=(path.steps,),
        in_specs=mixer_specs + [path.rows(D_MODEL), mod(2), _layer((D_MODEL, D_MODEL), j),
                                mod(3), mod(4), mod(5), _const((1, D_MODEL)),
                                _layer((D_MODEL, 2 * D_FF), layer), _layer((D_FF, D_MODEL), layer),
                                _const((1, D_MODEL))],
        out_specs=path.rows(D_MODEL),
        out_shape=jax.ShapeDtypeStruct((path.n, D_MODEL), F32),
        scratch_shapes=[pltpu.VMEM((path.tm, D_MODEL), F32)],
        compiler_params=_params(("arbitrary",)),
        name="tail_b" if gated_heads else "tail",
    )(*mixer_out, x, mods, w_o, mods, mods, mods, g, w_in, w_out, g_final)


def _prep_weights(p):
    w = {}
    w['a_qkv'] = p['w_a_qkv'].astype(BF16)
    w['a_o'] = p['w_a_o'].astype(BF16)
    nqk, nv = H_B * DQK_B, H_B * DV_B
    w_b = p['w_b_in']
    gate_pad = jnp.zeros(w_b.shape[:2] + (LANES - 2 * H_B,), w_b.dtype)
    w['b_in'] = jnp.concatenate([w_b, gate_pad], axis=-1).astype(BF16)
    bg = p['b_b_gates']
    w['b_gates'] = jnp.concatenate([bg, jnp.zeros((bg.shape[0], LANES - 2 * H_B), bg.dtype)], axis=-1)[:, None, :]
    w['b_out'] = p['w_b_out'].astype(BF16)
    n_c = p['w_c_dq'].shape[0]
    w_dkv = p['w_c_dkv']
    w['c_1'] = jnp.concatenate([p['w_c_dq'], w_dkv, jnp.zeros((n_c, D_MODEL, LANES - ROPE_C), w_dkv.dtype)],
                               axis=-1).astype(BF16)
    w_uq = p['w_c_uq'].reshape(n_c, Q_LORA, H_C, NOPE_C + ROPE_C)
    w_uq = jnp.concatenate([w_uq, jnp.zeros((n_c, Q_LORA, H_C, DQK_C_PAD - NOPE_C - ROPE_C), w_uq.dtype)], axis=-1)
    w['c_uq'] = w_uq.reshape(n_c, Q_LORA, H_C * DQK_C_PAD).astype(BF16)
    w['c_uq_t'] = jnp.swapaxes(w['c_uq'], 1, 2)
    w['c_ukv'] = p['w_c_ukv'].astype(BF16)
    w_ukv = w['c_ukv'].reshape(n_c, KV_LORA, H_C, NOPE_C + V_C)
    w['c_uk'] = w_ukv[..., :NOPE_C].reshape(n_c, KV_LORA, H_C * NOPE_C)
    w['c_uv_t'] = jnp.swapaxes(w_ukv[..., NOPE_C:].reshape(n_c, KV_LORA, H_C * V_C), 1, 2)
    w['c_o'] = p['w_c_o'].astype(BF16)
    w['ffn_in'] = p['w_ffn_in'].astype(BF16)
    w['ffn_out'] = p['w_ffn_out'].astype(BF16)
    return w


def _trunk(x, mods, past, p, w):
    batch, seq, _ = x.shape
    path = _Path(batch, seq)
    n_past = 0 if past is None else past[5].shape[2]
    pos = n_past + jnp.arange(seq, dtype=jnp.int32)
    tab_a = tuple(path.expand_table(t) for t in _rope_tables(pos, ROT_A, D_SUB))
    tab_c = tuple(path.expand_table(t) for t in _rope_tables(pos, ROPE_C, LANES))
    tab_c_t = _rope_tables_t(pos, ROPE_C)
    x = x.reshape(path.n, D_MODEL)
    g_final = p['g_final'][None, :]
    n_a = (DEPTH + N_MIXERS - 1) // N_MIXERS
    a_rows = None
    b_c, b_n, b_m, c_kv, c_kr = [], [], [], [], []
    mods = path.expand_mod(mods)
    for i in range(DEPTH):
        kind, j = i % N_MIXERS, i // N_MIXERS
        g1 = p['g_norm1'][i][None, :]
        if kind == 0:
            lambda_init = 0.8 - 0.6 * math.exp(-0.3 * i)
            q, k, v, k_row, v_row = _proj_a(path, x, mods, i, g1, w['a_qkv'], tab_a, j, n_a, a_rows)
            a_rows = (k_row, v_row)
            shape3 = (batch, seq, D_MODEL)
            if past is None:
                g_sub_t = jnp.broadcast_to(p['g_a_sub'][:, :, None], p['g_a_sub'].shape + (TB,))
                o = _flash(q, k.reshape(shape3), v, batch, seq, H_A, LANES,
                           diff_args=(p['a_lambda'], g_sub_t, j), lambda_init=lambda_init)
            else:
                o = _decode_a(q.reshape(shape3), k.reshape(shape3), v.reshape(shape3), past[0], past[1],
                              p['a_lambda'], p['g_a_sub'][:, None, :], j, lambda_init, batch, seq)
            mixer_out, w_o = (o.reshape(path.n, D_MODEL),), w['a_o']
        elif kind == 1:
            q, k, kt, v, opre, gates, gt = _proj_b(path, x, mods, i, g1, j, w['b_in'], w['b_gates'][j])
            nqk, nv = H_B * DQK_B, H_B * DV_B
            if path.table_rows != seq:
                kt = kt.reshape(nqk, batch, seq).transpose(1, 0, 2)
                gt = gt.reshape(2 * H_B, batch, seq).transpose(1, 0, 2)
            if past is None:
                c0 = jnp.zeros((batch, H_B, DQK_B, DV_B), F32)
                n0 = jnp.zeros((batch, H_B, 1, DQK_B), F32)
                m0 = jnp.zeros((batch, H_B, 1, 1), F32)
            else:
                c0 = past[2][j]
                n0 = past[3][j][:, :, None, :]
                m0 = past[4][j][:, :, None, None]
            hs, c_t, n_t, m_t = _mlstm(q.reshape(batch, seq, nqk), k.reshape(batch, seq, nqk), kt,
                                       v.reshape(batch, seq, nv), gates.reshape(batch, seq, LANES), gt,
                                       c0, n0, m0, batch, seq)
            b_c.append(c_t)
            b_n.append(n_t[:, :, 0, :])
            b_m.append(m_t[:, :, 0, 0])
            mixer_out, w_o = (hs.reshape(path.n, nv), opre, p['g_b_out'][j][None, :]), w['b_out']
        else:
            q, k, v, kv_row, kr_row = _proj_c(path, x, mods, i, g1, j, w, p['g_c_q'][j][None, :],
                                              p['g_c_kv'][j][None, :], tab_c, tab_c_t)
            qk_shape = (batch, seq, H_C * DQK_C_PAD)
            v_shape = (batch, seq, H_C * V_C)
            if past is None:
                o = _flash(q, k.reshape(qk_shape), v, batch, seq, H_C, DQK_C_PAD)
            else:
                o = _decode_c(q.reshape(qk_shape), k.reshape(qk_shape), v.reshape(v_shape), past[5], past[6],
                              w['c_ukv'], j, batch, seq)
            c_kv.append(kv_row.reshape(batch, seq, KV_LORA))
            c_kr.append(kr_row.reshape(batch, seq, ROPE_C))
            mixer_out, w_o = (o.reshape(path.n, D_MODEL),), w['c_o']
        x = _tail(path, mixer_out, x, mods, i, w_o, j, p['g_norm2'][i][None, :],
                  w['ffn_in'], w['ffn_out'], g_final, final=(i == DEPTH - 1))
    y = x.reshape(batch, seq, D_MODEL)
    a_k, a_v = (r.reshape(n_a, batch, seq, H_A, 2 * D_SUB) for r in a_rows)
    return y, (a_k, a_v, jnp.stack(b_c), jnp.stack(b_n), jnp.stack(b_m), jnp.stack(c_kv), jnp.stack(c_kr))


def kernel(x_prompt, x_sample, c_prompt, c_sample, cache_a_k, cache_a_v, state_b_c, state_b_n, state_b_m, cache_c_kv, cache_c_kr, w_ada, b_ada, g_norm1, g_norm2, w_a_qkv, a_lambda, g_a_sub, w_a_o, w_b_in, b_b_gates, g_b_out, w_b_out, w_c_dq, g_c_q, w_c_uq, w_c_dkv, g_c_kv, w_c_ukv, w_c_o, w_ffn_in, w_ffn_out, g_final):
    p = dict(g_norm1=g_norm1, g_norm2=g_norm2, w_a_qkv=w_a_qkv, a_lambda=a_lambda, g_a_sub=g_a_sub,
             w_a_o=w_a_o, w_b_in=w_b_in, b_b_gates=b_b_gates, g_b_out=g_b_out, w_b_out=w_b_out,
             w_c_dq=w_c_dq, g_c_q=g_c_q, w_c_uq=w_c_uq, w_c_dkv=w_c_dkv, g_c_kv=g_c_kv, w_c_ukv=w_c_ukv,
             w_c_o=w_c_o, w_ffn_in=w_ffn_in, w_ffn_out=w_ffn_out, g_final=g_final)
    w = _prep_weights(p)
    nb_p = x_prompt.shape[0]
    mods = _ada(jnp.concatenate([c_prompt, c_sample], axis=0), w_ada, b_ada)
    y_prompt, sp = _trunk(x_prompt, mods[:, :nb_p], None, p, w)
    past = (cache_a_k.reshape(-1, 2 * D_SUB), cache_a_v.reshape(-1, 2 * D_SUB),
            state_b_c, state_b_n, state_b_m, cache_c_kv, cache_c_kr)
    y_sample, ss = _trunk(x_sample, mods[:, nb_p:], past, p, w)
    return (y_prompt, y_sample) + sp + ss
```
